```python
import jax
import jax.numpy as jnp
from jax import lax
import numpy as np


D_MODEL = 1024
BATCH = 4
SEQ = 4096
DEPTH = 1
DEC_BATCH = 2
DEC_SEQ = 16384
PAST_LEN = 128

GRID_W = 64
PLE_DIM = 256
NORM_EPS = 1e-6
DN_HEADS = 4
DN_HEAD_DIM = 128
DN_WIDTH = DN_HEADS * DN_HEAD_DIM
CONV_K = 5
CHUNK = 64
ATT_HEADS = 8
ATT_KV_HEADS = 2
ATT_HEAD_DIM = 64
ATT_WIDTH = ATT_HEADS * ATT_HEAD_DIM
ATT_KV_WIDTH = ATT_KV_HEADS * ATT_HEAD_DIM
Q_BLOCK = 128
ROPE_THETA = 10000.0
MIX_WIDTH = DN_WIDTH + ATT_WIDTH
IN_SIZES = (3 * DN_WIDTH, DN_WIDTH, DN_HEADS, DN_HEADS, DN_HEADS, DN_HEADS, ATT_WIDTH, ATT_KV_WIDTH, ATT_KV_WIDTH)
IN_COLS = sum(IN_SIZES)
IN_SPLITS = tuple(sum(IN_SIZES[:i + 1]) for i in range(len(IN_SIZES) - 1))
N_KEYS = 128
N_EXPERTS = N_KEYS * N_KEYS
PEER_HEADS = 8
PEER_TOPK = 16
D_KEY = 256
D_KEY_HALF = D_KEY // 2
PEER_BLOCK = 128

kernel_name = 'hymba_deltanet_gqa_peer_encoder'


def rmsnorm(x, gain):
    x32 = x.astype(jnp.float32)
    y = x32 * lax.rsqrt(jnp.mean(x32 * x32, axis=-1, keepdims=True) + NORM_EPS)
    return (y * gain.astype(jnp.float32)).astype(x.dtype)


def l2norm(x):
    return x * lax.rsqrt(jnp.sum(x * x, axis=-1, keepdims=True) + NORM_EPS)


def short_conv(x, w):
    pad = CONV_K // 2
    return lax.conv_general_dilated(x, w[:, None, :].astype(x.dtype), window_strides=(1,), padding=[(pad, pad)], dimension_numbers=('NWC', 'WIO', 'NWC'), feature_group_count=x.shape[-1])


def chunk_gated_delta(q, k, v, g, beta):
    B, T, H, dk = q.shape
    dv = v.shape[-1]
    n_chunks = T // CHUNK
    q = q * (dk ** -0.5)

    def chunks(t):
        return jnp.moveaxis(t.reshape((B, n_chunks, CHUNK) + t.shape[2:]), 3, 1)

    q, k, v, g, beta = chunks(q), chunks(k), chunks(v), chunks(g), chunks(beta)
    gc = jnp.cumsum(g, axis=-1)
    tril = jnp.tril(jnp.ones((CHUNK, CHUNK), dtype=bool))
    strict = jnp.tril(jnp.ones((CHUNK, CHUNK), dtype=bool), -1)
    diff = gc[..., :, None] - gc[..., None, :]
    decay = jnp.where(tril, jnp.exp(jnp.where(tril, diff, 0.0)), 0.0)
    kb = k * beta[..., None]
    vb = v * beta[..., None]
    lower = jnp.where(strict, jnp.einsum('bhncd,bhnsd->bhncs', kb, k) * decay, 0.0)
    rhs = jnp.concatenate([vb, kb * jnp.exp(gc)[..., None]], axis=-1)
    sol = lax.linalg.triangular_solve(lower, rhs, left_side=True, lower=True, unit_diagonal=True)
    u, w = sol[..., :dv], sol[..., dv:]
    qk_intra = jnp.where(tril, jnp.einsum('bhncd,bhnsd->bhncs', q, k) * decay, 0.0)

    def step(S, xs):
        qi, ki, ui, wi, gi, ai = xs
        v_new = ui - jnp.einsum('bhck,bhkv->bhcv', wi, S)
        o = jnp.einsum('bhck,bhkv->bhcv', qi * jnp.exp(gi)[..., None], S) + jnp.einsum('bhcs,bhsv->bhcv', ai, v_new)
        g_last = gi[..., -1]
        S = S * jnp.exp(g_last)[..., None, None] + jnp.einsum('bhck,bhcv->bhkv', ki * jnp.exp(g_last[..., None] - gi)[..., None], v_new)
        return S, o

    xs = tuple(jnp.moveaxis(t, 2, 0) for t in (q, k, u, w, gc, qk_intra))
    S0 = jnp.zeros((B, H, dk, dv), dtype=jnp.float32)
    _, o = lax.scan(step, S0, xs)
    return jnp.transpose(o, (1, 0, 3, 2, 4)).reshape(B, T, H, dv)


def deltanet_group(qkv, z, b_f, b_b, a_f, a_b, conv_w, a_log_f, a_log_b, dt_bias_f, dt_bias_b, out_norm):
    B, T, _ = qkv.shape
    f32 = jnp.float32
    qkv = jax.nn.silu(short_conv(qkv, conv_w)).astype(f32)
    q, k, v = jnp.split(qkv, 3, axis=-1)
    q = l2norm(q.reshape(B, T, DN_HEADS, DN_HEAD_DIM))
    k = l2norm(k.reshape(B, T, DN_HEADS, DN_HEAD_DIM))
    v = v.reshape(B, T, DN_HEADS, DN_HEAD_DIM)
    g_f = -jnp.exp(a_log_f.astype(f32)) * jax.nn.softplus(a_f.astype(f32) + dt_bias_f.astype(f32))
    g_b = -jnp.exp(a_log_b.astype(f32)) * jax.nn.softplus(a_b.astype(f32) + dt_bias_b.astype(f32))
    beta_f = jax.nn.sigmoid(b_f.astype(f32))
    beta_b = jax.nn.sigmoid(b_b.astype(f32))
    o_f = chunk_gated_delta(q, k, v, g_f, beta_f)
    o_b = jnp.flip(chunk_gated_delta(jnp.flip(q, 1), jnp.flip(k, 1), jnp.flip(v, 1), jnp.flip(g_b, 1), jnp.flip(beta_b, 1)), 1)
    o = rmsnorm(o_f + o_b, out_norm) * jax.nn.silu(z.astype(f32).reshape(B, T, DN_HEADS, DN_HEAD_DIM))
    return o.reshape(B, T, DN_WIDTH).astype(z.dtype)


def rope_1d(x, pos):
    d = x.shape[-1]
    inv_freq = ROPE_THETA ** (-jnp.arange(0, d, 2, dtype=jnp.float32) / d)
    ang = pos[:, None] * inv_freq[None, :]
    cos = jnp.cos(ang)[None, :, None, :]
    sin = jnp.sin(ang)[None, :, None, :]
    x1, x2 = jnp.split(x, 2, axis=-1)
    return jnp.concatenate([x1 * cos - x2 * sin, x2 * cos + x1 * sin], axis=-1)


def axial_rope(x, row_pos, col_pos):
    xr, xc = jnp.split(x.astype(jnp.float32), 2, axis=-1)
    return jnp.concatenate([rope_1d(xr, row_pos), rope_1d(xc, col_pos)], axis=-1).astype(x.dtype)


def gqa_group(q, k, v, q_norm, k_norm):
    B, T, _ = q.shape
    rows = T // GRID_W
    row_pos = jnp.repeat(jnp.arange(rows, dtype=jnp.float32), GRID_W)
    col_pos = jnp.tile(jnp.arange(GRID_W, dtype=jnp.float32), rows)
    G = ATT_HEADS // ATT_KV_HEADS
    q = axial_rope(rmsnorm(q.reshape(B, T, ATT_HEADS, ATT_HEAD_DIM), q_norm), row_pos, col_pos)
    k = axial_rope(rmsnorm(k.reshape(B, T, ATT_KV_HEADS, ATT_HEAD_DIM), k_norm), row_pos, col_pos)
    v = v.reshape(B, T, ATT_KV_HEADS, ATT_HEAD_DIM)
    n_blocks = T // Q_BLOCK
    qb = jnp.moveaxis(q.reshape(B, n_blocks, Q_BLOCK, ATT_KV_HEADS, G, ATT_HEAD_DIM), 1, 0)
    scale = ATT_HEAD_DIM ** -0.5

    def attend(qi):
        s = jnp.einsum('bqkgd,bskd->bkgqs', qi, k).astype(jnp.float32) * scale
        p = jax.nn.softmax(s, axis=-1).astype(v.dtype)
        return jnp.einsum('bkgqs,bskd->bqkgd', p, v)

    o = lax.map(attend, qb)
    return jnp.moveaxis(o, 0, 1).reshape(B, T, ATT_WIDTH)


def peer(x, w_query, keys_a, keys_b, u_emb, v_emb):
    B, T, D = x.shape
    n_tok = B * T
    xf = x.reshape(n_tok, D)
    q = jnp.einsum('nd,dk->nk', xf, w_query).astype(jnp.float32).reshape(n_tok, PEER_HEADS, 2, D_KEY_HALF)
    s_a = jnp.einsum('nhd,kd->nhk', q[:, :, 0], keys_a.astype(jnp.float32))
    s_b = jnp.einsum('nhd,kd->nhk', q[:, :, 1], keys_b.astype(jnp.float32))
    va, ia = lax.top_k(s_a, PEER_TOPK)
    vb, ib = lax.top_k(s_b, PEER_TOPK)
    cand_s = (va[..., :, None] + vb[..., None, :]).reshape(n_tok, PEER_HEADS, PEER_TOPK * PEER_TOPK)
    cand_i = (ia[..., :, None] * N_KEYS + ib[..., None, :]).reshape(n_tok, PEER_HEADS, PEER_TOPK * PEER_TOPK)
    top_s, pos = lax.top_k(cand_s, PEER_TOPK)
    idx = jnp.take_along_axis(cand_i, pos, axis=-1).reshape(n_tok, PEER_HEADS * PEER_TOPK)
    gates = jax.nn.softmax(top_s, axis=-1).reshape(n_tok, PEER_HEADS * PEER_TOPK).astype(x.dtype)
    n_blk = n_tok // PEER_BLOCK

    def block(args):
        xb, ibk, gb = args
        h = jax.nn.gelu(jnp.einsum('pd,pkd->pk', xb, u_emb[ibk]), approximate=False)
        return jnp.einsum('pk,pkd->pd', gb * h, v_emb[ibk])

    out = lax.map(block, (xf.reshape(n_blk, PEER_BLOCK, D), idx.reshape(n_blk, PEER_BLOCK, -1), gates.reshape(n_blk, PEER_BLOCK, -1)))
    return out.reshape(B, T, D)


def encoder_layer(h, p_i, attn_norm, w_in, conv_w, a_log_fwd, a_log_bwd, dt_bias_fwd, dt_bias_bwd, dn_out_norm, q_norm, k_norm, w_out, ffn_norm, peer_query, peer_keys_a, peer_keys_b, peer_u, peer_v, ple_proj, ple_norm, ple_gate):
    a = rmsnorm(h, attn_norm)
    proj = jnp.einsum('btd,dc->btc', a, w_in)
    dn_qkv, dn_z, b_f, b_b, a_f, a_b, at_q, at_k, at_v = jnp.split(proj, IN_SPLITS, axis=-1)
    dn_out = deltanet_group(dn_qkv, dn_z, b_f, b_b, a_f, a_b, conv_w, a_log_fwd, a_log_bwd, dt_bias_fwd, dt_bias_bwd, dn_out_norm)
    at_out = gqa_group(at_q, at_k, at_v, q_norm, k_norm)
    mix = jnp.concatenate([dn_out, at_out], axis=-1)
    h = h + jnp.einsum('btc,cd->btd', mix, w_out)
    h = h + peer(rmsnorm(h, ffn_norm), peer_query, peer_keys_a, peer_keys_b, peer_u, peer_v)
    ple = rmsnorm(jnp.einsum('btp,pd->btd', p_i, ple_proj), ple_norm)
    h = h + ple * jax.nn.sigmoid(jnp.einsum('btd,de->bte', h, ple_gate))
    return h


def setup_inputs(seed: int = 0) -> dict:
    key = jax.random.key(seed)
    ks = jax.random.split(key, 32)
    f32 = jnp.float32

    def nrm(k, shape, scale):
        return jax.random.normal(k, shape, f32) * scale

    def gain(k, shape):
        return 1.0 + 0.02 * jax.random.normal(k, shape, f32)

    def inv_softplus_dt(k, shape):
        dt = jnp.exp(jax.random.uniform(k, shape, f32, minval=float(np.log(1e-3)), maxval=float(np.log(1e-1))))
        return dt + jnp.log(-jnp.expm1(-dt))

    return {
        'x_prompt': nrm(ks[0], (BATCH, SEQ, D_MODEL), 1.0),
        'x_sample': nrm(ks[1], (DEC_BATCH, DEC_SEQ, D_MODEL), 1.0),
        'p_prompt': nrm(ks[2], (DEPTH, BATCH, SEQ, PLE_DIM), 1.0),
        'p_sample': nrm(ks[3], (DEPTH, DEC_BATCH, DEC_SEQ, PLE_DIM), 1.0),
        'attn_norm': gain(ks[4], (DEPTH, D_MODEL)),
        'w_in': nrm(ks[5], (DEPTH, D_MODEL, IN_COLS), D_MODEL ** -0.5),
        'conv_w': nrm(ks[6], (DEPTH, CONV_K, 3 * DN_WIDTH), CONV_K ** -0.5),
        'a_log_fwd': jnp.log(jax.random.uniform(ks[7], (DEPTH, DN_HEADS), f32, minval=1.0, maxval=16.0)),
        'a_log_bwd': jnp.log(jax.random.uniform(ks[8], (DEPTH, DN_HEADS), f32, minval=1.0, maxval=16.0)),
        'dt_bias_fwd': inv_softplus_dt(ks[9], (DEPTH, DN_HEADS)),
        'dt_bias_bwd': inv_softplus_dt(ks[10], (DEPTH, DN_HEADS)),
        'dn_out_norm': gain(ks[11], (DEPTH, DN_HEAD_DIM)),
        'q_norm': gain(ks[12], (DEPTH, ATT_HEAD_DIM)),
        'k_norm': gain(ks[13], (DEPTH, ATT_HEAD_DIM)),
        'w_out': nrm(ks[14], (DEPTH, MIX_WIDTH, D_MODEL), MIX_WIDTH ** -0.5),
        'ffn_norm': gain(ks[15], (DEPTH, D_MODEL)),
        'peer_query': nrm(ks[16], (DEPTH, D_MODEL, PEER_HEADS * D_KEY), D_MODEL ** -0.5),
        'peer_keys_a': nrm(ks[17], (DEPTH, N_KEYS, D_KEY_HALF), D_KEY_HALF ** -0.5),
        'peer_keys_b': nrm(ks[18], (DEPTH, N_KEYS, D_KEY_HALF), D_KEY_HALF ** -0.5),
        'peer_u': nrm(ks[19], (DEPTH, N_EXPERTS, D_MODEL), D_MODEL ** -0.5),
        'peer_v': nrm(ks[20], (DEPTH, N_EXPERTS, D_MODEL), D_MODEL ** -0.5),
        'ple_proj': nrm(ks[21], (DEPTH, PLE_DIM, D_MODEL), PLE_DIM ** -0.5),
        'ple_norm': gain(ks[22], (DEPTH, D_MODEL)),
        'ple_gate': nrm(ks[23], (DEPTH, D_MODEL, D_MODEL), D_MODEL ** -0.5),
        'final_norm': gain(ks[24], (D_MODEL,)),
    }


def reference(x_prompt, x_sample, p_prompt, p_sample, attn_norm, w_in, conv_w, a_log_fwd, a_log_bwd, dt_bias_fwd, dt_bias_bwd, dn_out_norm, q_norm, k_norm, w_out, ffn_norm, peer_query, peer_keys_a, peer_keys_b, peer_u, peer_v, ple_proj, ple_norm, ple_gate, final_norm):
    h_p = x_prompt
    h_s = x_sample
    for i in range(DEPTH):
        layer_w = (attn_norm[i], w_in[i], conv_w[i], a_log_fwd[i], a_log_bwd[i], dt_bias_fwd[i], dt_bias_bwd[i], dn_out_norm[i], q_norm[i], k_norm[i], w_out[i], ffn_norm[i], peer_query[i], peer_keys_a[i], peer_keys_b[i], peer_u[i], peer_v[i], ple_proj[i], ple_norm[i], ple_gate[i])
        h_p = encoder_layer(h_p, p_prompt[i], *layer_w)
        h_s = encoder_layer(h_s, p_sample[i], *layer_w)
    y_prompt = rmsnorm(h_p, final_norm)
    y_sample = rmsnorm(h_s, final_norm)
    return (y_prompt, y_sample)
```

```python
import functools
import math

import jax
import jax.numpy as jnp
import numpy as np
from jax import lax
from jax.experimental import pallas as pl
from jax.experimental.pallas import tpu as pltpu
from jax.experimental.pallas import tpu_sc as plsc

F32 = jnp.float32
BF16 = jnp.bfloat16
HI = lax.Precision.HIGHEST

D_MODEL = 1024
GRID_W = 64
PLE_DIM = 256
NORM_EPS = 1e-6
DN_HEADS = 4
DN_HEAD_DIM = 128
DN_WIDTH = DN_HEADS * DN_HEAD_DIM
CONV_K = 5
CHUNK = 64
ATT_HEADS = 8
ATT_KV_HEADS = 2
ATT_GROUP = ATT_HEADS // ATT_KV_HEADS
ATT_HEAD_DIM = 64
ATT_WIDTH = ATT_HEADS * ATT_HEAD_DIM
ATT_KV_WIDTH = ATT_KV_HEADS * ATT_HEAD_DIM
ROPE_THETA = 10000.0
MIX_WIDTH = DN_WIDTH + ATT_WIDTH
N_KEYS = 128
PEER_HEADS = 8
PEER_TOPK = 16
D_KEY_HALF = 128
PEER_SEL = PEER_HEADS * PEER_TOPK
HALF_D = D_MODEL // 2

LANES = 128
SUBLANES = 8
SC_CORES = 2
SC_SUBCORES = 16
SC_WORKERS = SC_CORES * SC_SUBCORES
SC_ROWS = 64
VMEM_LIMIT = 56 * 1024 * 1024

GATE_PAD = LANES
IN_PAD_COLS = 3 * DN_WIDTH + DN_WIDTH + GATE_PAD + ATT_WIDTH + 2 * ATT_KV_WIDTH


def _params(sem):
    return pltpu.CompilerParams(dimension_semantics=sem, vmem_limit_bytes=VMEM_LIMIT)


def _rms(x, gain):
    return x * lax.rsqrt(jnp.mean(x * x, axis=-1, keepdims=True) + NORM_EPS) * gain


def _row_tile(n, want):
    t = min(n, want)
    assert n % t == 0
    return t


def _in_proj_kernel(x_ref, g_ref, w_ref, qkv_ref, z_ref, gates_ref, att_ref):
    a = _rms(x_ref[...], g_ref[...]).astype(BF16)
    y = jnp.dot(a, w_ref[...], preferred_element_type=F32)
    c0 = 3 * DN_WIDTH
    c1 = c0 + DN_WIDTH
    c2 = c1 + GATE_PAD
    qkv_ref[...] = y[:, :c0]
    z_ref[...] = y[:, c0:c1]
    gates_ref[...] = y[:, c1:c2]
    att_ref[...] = y[:, c2:]


def in_proj(x, gain, w_pad):
    n = x.shape[0]
    tm = _row_tile(n, 512)
    att_w = ATT_WIDTH + 2 * ATT_KV_WIDTH
    row = lambda w: pl.BlockSpec((tm, w), lambda i: (i, 0))
    return pl.pallas_call(
        _in_proj_kernel,
        grid=(n // tm,),
        in_specs=[row(D_MODEL), pl.BlockSpec((1, D_MODEL), lambda i: (0, 0)),
                  pl.BlockSpec((D_MODEL, IN_PAD_COLS), lambda i: (0, 0))],
        out_specs=[row(3 * DN_WIDTH), row(DN_WIDTH), row(GATE_PAD), row(att_w)],
        out_shape=[jax.ShapeDtypeStruct((n, 3 * DN_WIDTH), F32), jax.ShapeDtypeStruct((n, DN_WIDTH), F32),
                   jax.ShapeDtypeStruct((n, GATE_PAD), F32), jax.ShapeDtypeStruct((n, att_w), F32)],
        compiler_params=_params(("parallel",)),
        name="in_proj",
    )(x, gain, w_pad)


def _conv_kernel(prev_ref, cur_ref, next_ref, w_ref, o_ref, *, tt, n_t):
    t = pl.program_id(1)
    c = pl.program_id(2)
    cur = cur_ref[0]
    prev = jnp.where(t > 0, prev_ref[0], 0.0)
    nxt = jnp.where(t < n_t - 1, next_ref[0], 0.0)
    xe = jnp.concatenate([prev, cur, nxt], axis=0)
    w = w_ref[...]
    pad = CONV_K // 2
    acc = w[0:1] * xe[SUBLANES - pad:SUBLANES - pad + tt]
    for j in range(1, CONV_K):
        lo = SUBLANES - pad + j
        acc = acc + w[j:j + 1] * xe[lo:lo + tt]
    y = acc * jax.nn.sigmoid(acc)
    nrm = y * lax.rsqrt(jnp.sum(y * y, axis=-1, keepdims=True) + NORM_EPS)
    nrm = nrm * jnp.where(c < DN_HEADS, DN_HEAD_DIM ** -0.5, 1.0)
    o_ref[0] = jnp.where(c < 2 * DN_HEADS, nrm, y)


def dn_conv(qkv, conv_w_pad):
    B, T, C = qkv.shape
    tt = _row_tile(T, 1024)
    n_t = T // tt
    r8 = tt // SUBLANES
    last8 = T // SUBLANES - 1
    return pl.pallas_call(
        functools.partial(_conv_kernel, tt=tt, n_t=n_t),
        grid=(B, n_t, C // LANES),
        in_specs=[
            pl.BlockSpec((1, SUBLANES, LANES), lambda b, t, c: (b, jnp.maximum(t * r8 - 1, 0), c)),
            pl.BlockSpec((1, tt, LANES), lambda b, t, c: (b, t, c)),
            pl.BlockSpec((1, SUBLANES, LANES), lambda b, t, c: (b, jnp.minimum((t + 1) * r8, last8), c)),
            pl.BlockSpec((SUBLANES, LANES), lambda b, t, c: (0, c)),
        ],
        out_specs=pl.BlockSpec((1, tt, LANES), lambda b, t, c: (b, t, c)),
        out_shape=jax.ShapeDtypeStruct((B, T, C), F32),
        compiler_params=_params(("parallel", "parallel", "parallel")),
        name="dn_conv",
    )(qkv, qkv, qkv, conv_w_pad)


def _delta_kernel(qkv_ref, gates_ref, alog_ref, dtb_ref, o_ref, s_ref, gc_ref, beta_ref, *, tb, reverse):
    n_chunks = tb // CHUNK

    @pl.when(pl.program_id(1) == 0)
    def _():
        s_ref[...] = jnp.zeros_like(s_ref)

    gates = gates_ref[0]
    x = gates + dtb_ref[...]
    softplus = jnp.maximum(x, 0.0) + jnp.log1p(jnp.exp(-jnp.abs(x)))
    g = -jnp.exp(alog_ref[...]) * softplus
    beta_ref[...] = jax.nn.sigmoid(gates)
    r = lax.broadcasted_iota(jnp.int32, (tb, tb), 0)
    c = lax.broadcasted_iota(jnp.int32, (tb, tb), 1)
    same = (r // CHUNK) == (c // CHUNK)
    tri = (c >= r) if reverse else (c <= r)
    cum = jnp.where(same & tri, 1.0, 0.0).astype(F32)
    gc_ref[...] = jnp.dot(cum, g, preferred_element_type=F32, precision=HI)

    ri = lax.broadcasted_iota(jnp.int32, (CHUNK, CHUNK), 0)
    ci = lax.broadcasted_iota(jnp.int32, (CHUNK, CHUNK), 1)
    incl = (ci >= ri) if reverse else (ci <= ri)
    strict = (ci > ri) if reverse else (ci < ri)
    eye = ri == ci
    ones = jnp.ones((CHUNK, CHUNK), F32)
    ident = jnp.where(eye, 1.0, 0.0).astype(F32)
    beta_lane0 = DN_HEADS if reverse else 0
    g_lane0 = 3 * DN_HEADS if reverse else 2 * DN_HEADS
    nt = (((1,), (1,)), ((), ()))
    tn = (((0,), (0,)), ((), ()))

    def chunk_step(i, carry):
        ch = (n_chunks - 1 - i) if reverse else i
        r0 = pl.multiple_of(ch * CHUNK, CHUNK)
        gc_all = gc_ref[pl.ds(r0, CHUNK), :]
        beta_all = beta_ref[pl.ds(r0, CHUNK), :]
        for h in range(DN_HEADS):
            q = qkv_ref[0, pl.ds(r0, CHUNK), h * DN_HEAD_DIM:(h + 1) * DN_HEAD_DIM]
            k = qkv_ref[0, pl.ds(r0, CHUNK), (DN_HEADS + h) * DN_HEAD_DIM:(DN_HEADS + h + 1) * DN_HEAD_DIM]
            v = qkv_ref[0, pl.ds(r0, CHUNK), (2 * DN_HEADS + h) * DN_HEAD_DIM:(2 * DN_HEADS + h + 1) * DN_HEAD_DIM]
            gc = gc_all[:, g_lane0 + h:g_lane0 + h + 1]
            beta = beta_all[:, beta_lane0 + h:beta_lane0 + h + 1]
            gc_row = jnp.dot(ones, jnp.where(eye, gc, 0.0), preferred_element_type=F32, precision=HI)
            decay = jnp.where(incl, jnp.exp(jnp.where(incl, gc - gc_row, 0.0)), 0.0)
            kb = k * beta
            vb = v * beta
            kk = lax.dot_general(kb, k, nt, preferred_element_type=F32, precision=HI)
            neg = -jnp.where(strict, kk * decay, 0.0)
            inv = ident + neg
            pw = neg
            for _ in range(int(math.log2(CHUNK)) - 1):
                pw = jnp.dot(pw, pw, preferred_element_type=F32, precision=HI)
                inv = jnp.dot(inv, ident + pw, preferred_element_type=F32, precision=HI)
            eg = jnp.exp(gc)
            u = jnp.dot(inv, vb, preferred_element_type=F32, precision=HI)
            w = jnp.dot(inv, kb * eg, preferred_element_type=F32, precision=HI)
            a = jnp.where(incl, lax.dot_general(q, k, nt, preferred_element_type=F32, precision=HI) * decay, 0.0)
            s = s_ref[h]
            v_new = u - jnp.dot(w, s, preferred_element_type=F32, precision=HI)
            o = (jnp.dot(q * eg, s, preferred_element_type=F32, precision=HI)
                 + jnp.dot(a, v_new, preferred_element_type=F32, precision=HI))
            g_last = gc[0:1] if reverse else gc[CHUNK - 1:CHUNK]
            s_ref[h] = s * jnp.exp(g_last) + lax.dot_general(
                k * jnp.exp(g_last - gc), v_new, tn, preferred_element_type=F32, precision=HI)
            o_ref[0, pl.ds(r0, CHUNK), h * DN_HEAD_DIM:(h + 1) * DN_HEAD_DIM] = o
        return carry

    lax.fori_loop(0, n_chunks, chunk_step, 0)


def delta_scan(qkvc, gates, alog_row, dtb_row, reverse):
    B, T, C = qkvc.shape
    tb = _row_tile(T, 512)
    n_b = T // tb
    blk = (lambda b, i: (b, n_b - 1 - i, 0)) if reverse else (lambda b, i: (b, i, 0))
    return pl.pallas_call(
        functools.partial(_delta_kernel, tb=tb, reverse=reverse),
        grid=(B, n_b),
        in_specs=[pl.BlockSpec((1, tb, C), blk), pl.BlockSpec((1, tb, GATE_PAD), blk),
                  pl.BlockSpec((1, GATE_PAD), lambda b, i: (0, 0)), pl.BlockSpec((1, GATE_PAD), lambda b, i: (0, 0))],
        out_specs=pl.BlockSpec((1, tb, DN_WIDTH), blk),
        out_shape=jax.ShapeDtypeStruct((B, T, DN_WIDTH), F32),
        scratch_shapes=[pltpu.VMEM((DN_HEADS, DN_HEAD_DIM, DN_HEAD_DIM), F32),
                        pltpu.VMEM((tb, GATE_PAD), F32), pltpu.VMEM((tb, GATE_PAD), F32)],
        compiler_params=_params(("parallel", "arbitrary")),
        name="delta_bwd" if reverse else "delta_fwd",
    )(qkvc, gates, alog_row, dtb_row)


def _rope(x, cos, sin_signed, first_half):
    partner = jnp.where(first_half, pltpu.roll(x, x.shape[-1] - ATT_HEAD_DIM // 4, axis=1),
                        pltpu.roll(x, ATT_HEAD_DIM // 4, axis=1))
    return x * cos + partner * sin_signed


def _attn_prep_kernel(att_ref, qn_ref, kn_ref, cos_ref, sin_ref, bd_ref, q_ref, k_ref, v_ref):
    att = att_ref[0]
    q = att[:, :ATT_WIDTH]
    k = att[:, ATT_WIDTH:ATT_WIDTH + ATT_KV_WIDTH]
    v = att[:, ATT_WIDTH + ATT_KV_WIDTH:]
    bd = bd_ref[...]
    q_ms = jnp.dot(q * q, bd, preferred_element_type=F32, precision=HI)
    k_ms = jnp.dot(k * k, bd[:ATT_KV_WIDTH, :ATT_KV_WIDTH], preferred_element_type=F32, precision=HI)
    qn = q * lax.rsqrt(q_ms + NORM_EPS) * qn_ref[...]
    kn = k * lax.rsqrt(k_ms + NORM_EPS) * kn_ref[...]
    cos = cos_ref[...]
    sin = sin_ref[...]
    lane = lax.broadcasted_iota(jnp.int32, cos.shape, 1)
    first_half = (lane % (ATT_HEAD_DIM // 2)) < (ATT_HEAD_DIM // 4)
    kr = _rope(kn, cos, sin, first_half)
    for h in range(ATT_KV_HEADS):
        k_ref[0, h] = kr[:, h * ATT_HEAD_DIM:(h + 1) * ATT_HEAD_DIM].astype(BF16)
        v_ref[0, h] = v[:, h * ATT_HEAD_DIM:(h + 1) * ATT_HEAD_DIM].astype(BF16)
    scale = ATT_HEAD_DIM ** -0.5
    for j in range(ATT_WIDTH // LANES):
        qr = _rope(qn[:, j * LANES:(j + 1) * LANES], cos, sin, first_half) * scale
        for e in range(LANES // ATT_HEAD_DIM):
            h = j * (LANES // ATT_HEAD_DIM) + e
            q_ref[0, h] = qr[:, e * ATT_HEAD_DIM:(e + 1) * ATT_HEAD_DIM].astype(BF16)


def attn_prep(att, q_gain, k_gain, cos_t, sin_t, bd):
    B, T, W = att.shape
    tt = _row_tile(T, 512)
    head = lambda nh: pl.BlockSpec((1, nh, tt, ATT_HEAD_DIM), lambda b, t: (b, 0, t, 0))
    const = lambda shape: pl.BlockSpec(shape, lambda b, t: (0, 0))
    return pl.pallas_call(
        _attn_prep_kernel,
        grid=(B, T // tt),
        in_specs=[pl.BlockSpec((1, tt, W), lambda b, t: (b, t, 0)), const((1, ATT_WIDTH)), const((1, ATT_KV_WIDTH)),
                  pl.BlockSpec((tt, LANES), lambda b, t: (t, 0)), pl.BlockSpec((tt, LANES), lambda b, t: (t, 0)),
                  const((ATT_WIDTH, ATT_WIDTH))],
        out_specs=[head(ATT_HEADS), head(ATT_KV_HEADS), head(ATT_KV_HEADS)],
        out_shape=[jax.ShapeDtypeStruct((B, ATT_HEADS, T, ATT_HEAD_DIM), BF16),
                   jax.ShapeDtypeStruct((B, ATT_KV_HEADS, T, ATT_HEAD_DIM), BF16),
                   jax.ShapeDtypeStruct((B, ATT_KV_HEADS, T, ATT_HEAD_DIM), BF16)],
        compiler_params=_params(("parallel", "parallel")),
        name="attn_prep",
    )(att, q_gain, k_gain, cos_t, sin_t, bd)


def _flash_kernel(q_ref, k_ref, v_ref, o_ref, m_ref, l_ref, acc_ref, *, tq):
    j = pl.program_id(3)

    @pl.when(j == 0)
    def _():
        m_ref[...] = jnp.full_like(m_ref, -jnp.inf)
        l_ref[...] = jnp.zeros_like(l_ref)
        acc_ref[...] = jnp.zeros_like(acc_ref)

    q = q_ref[0].reshape(ATT_GROUP * tq, ATT_HEAD_DIM)
    s = lax.dot_general(q, k_ref[0, 0], (((1,), (1,)), ((), ())), preferred_element_type=F32)
    m_prev = m_ref[...]
    m_new = jnp.maximum(m_prev, jnp.max(s, axis=-1, keepdims=True))
    alpha = jnp.exp(m_prev - m_new)
    p = jnp.exp(s - m_new)
    l_ref[...] = alpha * l_ref[...] + jnp.sum(p, axis=-1, keepdims=True)
    acc_ref[...] = alpha * acc_ref[...] + jnp.dot(p.astype(BF16), v_ref[0, 0], preferred_element_type=F32)
    m_ref[...] = m_new

    @pl.when(j == pl.num_programs(3) - 1)
    def _():
        o = acc_ref[...] / l_ref[...]
        o_ref[0] = jnp.concatenate([o[g * tq:(g + 1) * tq] for g in range(ATT_GROUP)], axis=-1)


def flash_attention(q, k, v):
    B, _, T, _ = q.shape
    tq = _row_tile(T, 256)
    tk = _row_tile(T, 1024)
    rows = ATT_GROUP * tq
    return pl.pallas_call(
        functools.partial(_flash_kernel, tq=tq),
        grid=(B, ATT_KV_HEADS, T // tq, T // tk),
        in_specs=[pl.BlockSpec((1, ATT_GROUP, tq, ATT_HEAD_DIM), lambda b, h, i, j: (b, h, i, 0)),
                  pl.BlockSpec((1, 1, tk, ATT_HEAD_DIM), lambda b, h, i, j: (b, h, j, 0)),
                  pl.BlockSpec((1, 1, tk, ATT_HEAD_DIM), lambda b, h, i, j: (b, h, j, 0))],
        out_specs=pl.BlockSpec((1, tq, ATT_GROUP * ATT_HEAD_DIM), lambda b, h, i, j: (b, i, h)),
        out_shape=jax.ShapeDtypeStruct((B, T, ATT_WIDTH), F32),
        scratch_shapes=[pltpu.VMEM((rows, 1), F32), pltpu.VMEM((rows, 1), F32), pltpu.VMEM((rows, ATT_HEAD_DIM), F32)],
        compiler_params=_params(("parallel", "parallel", "parallel", "arbitrary")),
        name="flash_attention",
    )(q, k, v)


def _out_proj_kernel(x_ref, of_ref, ob_ref, z_ref, at_ref, gn_ref, w_ref, h_ref):
    o = of_ref[...] + ob_ref[...]
    z = z_ref[...]
    gn = gn_ref[...]
    parts = []
    for h in range(DN_HEADS):
        sl = slice(h * DN_HEAD_DIM, (h + 1) * DN_HEAD_DIM)
        zh = z[:, sl]
        parts.append(_rms(o[:, sl], gn) * (zh * jax.nn.sigmoid(zh)))
    parts.append(at_ref[...])
    mix = jnp.concatenate(parts, axis=-1).astype(BF16)
    h_ref[...] = x_ref[...] + jnp.dot(mix, w_ref[...], preferred_element_type=F32)


def out_proj(x, o_f, o_b, z, at, gn, w_out):
    n = x.shape[0]
    tm = _row_tile(n, 512)
    row = lambda w: pl.BlockSpec((tm, w), lambda i: (i, 0))
    return pl.pallas_call(
        _out_proj_kernel,
        grid=(n // tm,),
        in_specs=[row(D_MODEL), row(DN_WIDTH), row(DN_WIDTH), row(DN_WIDTH), row(ATT_WIDTH),
                  pl.BlockSpec((1, DN_HEAD_DIM), lambda i: (0, 0)), pl.BlockSpec((MIX_WIDTH, D_MODEL), lambda i: (0, 0))],
        out_specs=row(D_MODEL),
        out_shape=jax.ShapeDtypeStruct((n, D_MODEL), F32),
        compiler_params=_params(("parallel",)),
        name="out_proj",
    )(x, o_f, o_b, z, at, gn, w_out)


def _top_rows(s, ids, k):
    rows = lax.broadcasted_iota(jnp.int32, s.shape, 0)
    big = jnp.int32(s.shape[0])
    vals, outs = [], []
    for _ in range(k):
        m = jnp.max(s, axis=0, keepdims=True)
        pos = jnp.min(jnp.where(s == m, rows, big), axis=0, keepdims=True)
        hit = rows == pos
        vals.append(m)
        if ids is None:
            outs.append(pos)
        else:
            outs.append(jnp.max(jnp.where(hit, ids, -1), axis=0, keepdims=True))
        s = jnp.where(hit, -jnp.inf, s)
    return jnp.concatenate(vals, axis=0), jnp.concatenate(outs, axis=0)


def _stair_counts():
    return [PEER_TOPK // (i + 1) for i in range(PEER_TOPK)]


def _route_kernel(h_ref, g_ref, wq_ref, ka_ref, kb_ref, idx_ref, gate_ref):
    xn = _rms(h_ref[...], g_ref[...]).astype(BF16)
    q = jnp.dot(xn, wq_ref[...], preferred_element_type=F32).astype(BF16)
    nt = (((1,), (1,)), ((), ()))
    counts = _stair_counts()
    n_cand = sum(counts)
    pad = (-n_cand) % SUBLANES
    tt = q.shape[0]
    for h in range(PEER_HEADS):
        qa = q[:, (2 * h) * D_KEY_HALF:(2 * h + 1) * D_KEY_HALF]
        qb = q[:, (2 * h + 1) * D_KEY_HALF:(2 * h + 2) * D_KEY_HALF]
        sa = lax.dot_general(ka_ref[...], qa, nt, preferred_element_type=F32)
        sb = lax.dot_general(kb_ref[...], qb, nt, preferred_element_type=F32)
        va, ia = _top_rows(sa, None, PEER_TOPK)
        vb, ib = _top_rows(sb, None, PEER_TOPK)
        cs, ci = [], []
        for i, cnt in enumerate(counts):
            cs.append(va[i:i + 1] + vb[:cnt])
            ci.append(ia[i:i + 1] * N_KEYS + ib[:cnt])
        if pad:
            cs.append(jnp.full((pad, tt), -jnp.inf, F32))
            ci.append(jnp.zeros((pad, tt), jnp.int32))
        top_s, top_i = _top_rows(jnp.concatenate(cs, axis=0), jnp.concatenate(ci, axis=0), PEER_TOPK)
        e = jnp.exp(top_s - top_s[0:1])
        gate_ref[0, h * PEER_TOPK:(h + 1) * PEER_TOPK, :] = e / jnp.sum(e, axis=0, keepdims=True)
        idx_ref[0, h * PEER_TOPK:(h + 1) * PEER_TOPK, :] = top_i


def peer_route(h1, gain, wq, keys_a, keys_b):
    n = h1.shape[0]
    tt = _row_tile(n, 256)
    const = lambda shape: pl.BlockSpec(shape, lambda i: (0, 0))
    out = pl.BlockSpec((1, PEER_SEL, tt), lambda i: (i, 0, 0))
    return pl.pallas_call(
        _route_kernel,
        grid=(n // tt,),
        in_specs=[pl.BlockSpec((tt, D_MODEL), lambda i: (i, 0)), const((1, D_MODEL)),
                  const((D_MODEL, PEER_HEADS * 2 * D_KEY_HALF)), const((N_KEYS, D_KEY_HALF)), const((N_KEYS, D_KEY_HALF))],
        out_specs=[out, out],
        out_shape=[jax.ShapeDtypeStruct((n // tt, PEER_SEL, tt), jnp.int32),
                   jax.ShapeDtypeStruct((n // tt, PEER_SEL, tt), F32)],
        compiler_params=_params(("parallel",)),
        name="peer_route",
    )(h1, gain, wq, keys_a, keys_b)


def pack_rows(t):
    b = lax.bitcast_convert_type(t.astype(BF16), jnp.uint16).astype(jnp.uint32)
    half = t.shape[1] // 2
    return b[:, :half] | (b[:, half:] << 16)


def sc_gather(table, idx):
    n_idx = idx.shape[0]
    _, width = table.shape
    per_w = n_idx // SC_WORKERS
    n_it = per_w // (2 * SC_ROWS)
    assert per_w * SC_WORKERS == n_idx and n_it * 2 * SC_ROWS == per_w and n_it >= 1
    mesh = plsc.VectorSubcoreMesh(core_axis_name="c", subcore_axis_name="s")

    @functools.partial(
        pl.kernel, mesh=mesh,
        out_type=jax.ShapeDtypeStruct((n_idx, width), table.dtype),
        scratch_types=[pltpu.VMEM((per_w,), jnp.int32),
                       pltpu.VMEM((SC_ROWS, width), table.dtype), pltpu.VMEM((SC_ROWS, width), table.dtype),
                       pltpu.SemaphoreType.DMA, pltpu.SemaphoreType.DMA,
                       pltpu.SemaphoreType.DMA, pltpu.SemaphoreType.DMA],
        name="peer_gather",
    )
    def k(table_hbm, idx_hbm, out_hbm, idx_v, buf0, buf1, g0, g1, s0, s1):
        base = (lax.axis_index("s") * SC_CORES + lax.axis_index("c")) * per_w
        pltpu.sync_copy(idx_hbm.at[pl.ds(base, per_w)], idx_v)
        bufs, gsem, ssem = (buf0, buf1), (g0, g1), (s0, s1)

        def gather(c, b):
            return pltpu.make_async_copy(table_hbm.at[idx_v.at[pl.ds(c * SC_ROWS, SC_ROWS)]], bufs[b], gsem[b])

        def put(c, b):
            return pltpu.make_async_copy(bufs[b], out_hbm.at[pl.ds(base + c * SC_ROWS, SC_ROWS)], ssem[b])

        @pl.loop(0, n_it)
        def _(i):
            for b in range(2):
                @pl.when(i > 0)
                def _():
                    put(2 * i + b - 2, b).wait()

                gather(2 * i + b, b).start()
            for b in range(2):
                gather(2 * i + b, b).wait()
                put(2 * i + b, b).start()

        for b in range(2):
            put(2 * (n_it - 1) + b, b).wait()

    return k(table, idx)


def _expert_kernel(h_ref, g_ref, gu_ref, gv_ref, gate_ref, o_ref, *, tb):
    m2 = 2 * PEER_SEL * tb
    h1 = h_ref[...]
    xn = _rms(h1, g_ref[...]).astype(BF16)
    xs = jnp.concatenate([xn[:, :HALF_D], xn[:, HALF_D:]], axis=0)
    gu = pltpu.bitcast(gu_ref[0], BF16)
    gv = pltpu.bitcast(gv_ref[0], BF16)
    r = lax.dot_general(xs, gu, (((1,), (1,)), ((), ())), preferred_element_type=F32)
    row = lax.broadcasted_iota(jnp.int32, (2 * tb, m2), 0)
    col = lax.broadcasted_iota(jnp.int32, (2 * tb, m2), 1)
    mine = ((col // 2) % tb + (col % 2) * tb) == row
    s = jnp.sum(jnp.where(mine, r, 0.0), axis=0, keepdims=True)
    lane = lax.broadcasted_iota(jnp.int32, (1, m2), 1)
    pair = s + pltpu.roll(s, 1, axis=1)
    hsum = jnp.where(lane % 2 == 1, pair, pltpu.roll(pair, m2 - 1, axis=1))
    gelu = 0.5 * hsum * (1.0 + lax.erf(hsum * math.sqrt(0.5)))
    act = gelu * gate_ref[0]
    wm = jnp.where(mine, jnp.broadcast_to(act, (2 * tb, m2)), 0.0).astype(BF16)
    o = jnp.dot(wm, gv, preferred_element_type=F32)
    o_ref[...] = h1 + jnp.concatenate([o[:tb], o[tb:]], axis=-1)


def peer_experts(h1, gain, gu, gv, gate2, tb):
    n = h1.shape[0]
    m = PEER_SEL * tb
    return pl.pallas_call(
        functools.partial(_expert_kernel, tb=tb),
        grid=(n // tb,),
        in_specs=[pl.BlockSpec((tb, D_MODEL), lambda i: (i, 0)), pl.BlockSpec((1, D_MODEL), lambda i: (0, 0)),
                  pl.BlockSpec((1, m, HALF_D), lambda i: (i, 0, 0)), pl.BlockSpec((1, m, HALF_D), lambda i: (i, 0, 0)),
                  pl.BlockSpec((1, 1, 2 * m), lambda i: (i, 0, 0))],
        out_specs=pl.BlockSpec((tb, D_MODEL), lambda i: (i, 0)),
        out_shape=jax.ShapeDtypeStruct((n, D_MODEL), F32),
        compiler_params=_params(("parallel",)),
        name="peer_experts",
    )(h1, gain, gu, gv, gate2)


def _ple_kernel(h_ref, p_ref, wp_ref, pn_ref, wg_ref, fn_ref, y_ref):
    h = h_ref[...]
    ple = _rms(jnp.dot(p_ref[...].astype(BF16), wp_ref[...], preferred_element_type=F32), pn_ref[...])
    gate = jax.nn.sigmoid(jnp.dot(h.astype(BF16), wg_ref[...], preferred_element_type=F32))
    y_ref[...] = _rms(h + ple * gate, fn_ref[...])


def ple_final(h2, p, wp, pn, wg, fn):
    n = h2.shape[0]
    tm = _row_tile(n, 512)
    row = lambda w: pl.BlockSpec((tm, w), lambda i: (i, 0))
    const = lambda shape: pl.BlockSpec(shape, lambda i: (0, 0))
    return pl.pallas_call(
        _ple_kernel,
        grid=(n // tm,),
        in_specs=[row(D_MODEL), row(PLE_DIM), const((PLE_DIM, D_MODEL)), const((1, D_MODEL)),
                  const((D_MODEL, D_MODEL)), const((1, D_MODEL))],
        out_specs=row(D_MODEL),
        out_shape=jax.ShapeDtypeStruct((n, D_MODEL), F32),
        compiler_params=_params(("parallel",)),
        name="ple_final",
    )(h2, p, wp, pn, wg, fn)


def _rope_tables(T):
    quarter = ATT_HEAD_DIM // 4
    inv_freq = ROPE_THETA ** (-jnp.arange(0, 2 * quarter, 2, dtype=F32) / (2 * quarter))
    t = jnp.arange(T)
    row_pos = (t // GRID_W).astype(F32)
    col_pos = (t % GRID_W).astype(F32)
    ang_r = row_pos[:, None] * inv_freq[None, :]
    ang_c = col_pos[:, None] * inv_freq[None, :]
    ang = jnp.concatenate([ang_r, ang_r, ang_c, ang_c], axis=-1)
    sign = jnp.tile(jnp.concatenate([-jnp.ones((quarter,), F32), jnp.ones((quarter,), F32)]), 2)
    reps = LANES // ATT_HEAD_DIM
    return jnp.tile(jnp.cos(ang), (1, reps)), jnp.tile(jnp.sin(ang) * sign, (1, reps))


def _gate_rows(a_log_f, a_log_b, dtb_f, dtb_b):
    zeros = jnp.zeros((2 * DN_HEADS,), F32)
    tail = jnp.zeros((GATE_PAD - 4 * DN_HEADS,), F32)
    alog = jnp.concatenate([zeros, a_log_f.astype(F32), a_log_b.astype(F32), tail])[None, :]
    dtb = jnp.concatenate([zeros, dtb_f.astype(F32), dtb_b.astype(F32), tail])[None, :]
    return alog, dtb


def _peer_chunk(n):
    for c in (4096, 2048, 1024, 512, 256, 128, 64):
        if n % c == 0:
            return c
    raise ValueError(n)


def _layer(x, p, wts, tb):
    B, T, D = x.shape
    n = B * T
    xf = x.reshape(n, D)
    qkv, z, gates, att = in_proj(xf, wts["attn_norm"], wts["w_in"])
    qkvc = dn_conv(qkv.reshape(B, T, -1), wts["conv_w"])
    gates3 = gates.reshape(B, T, -1)
    o_f = delta_scan(qkvc, gates3, wts["alog"], wts["dtb"], reverse=False)
    o_b = delta_scan(qkvc, gates3, wts["alog"], wts["dtb"], reverse=True)
    cos_t, sin_t = _rope_tables(T)
    q, k, v = attn_prep(att.reshape(B, T, -1), wts["q_norm"], wts["k_norm"], cos_t, sin_t, wts["head_avg"])
    at = flash_attention(q, k, v)
    h1 = out_proj(xf, o_f.reshape(n, -1), o_b.reshape(n, -1), z, at.reshape(n, -1), wts["dn_out_norm"], wts["w_out"])
    idx, gate = peer_route(h1, wts["ffn_norm"], wts["peer_query"], wts["keys_a"], wts["keys_b"])
    tt = idx.shape[-1]

    def per_block(a):
        a = a.reshape(n // tt, PEER_SEL, tt // tb, tb).transpose(0, 2, 1, 3)
        return a.reshape(n // tb, PEER_SEL * tb)

    idx_b = per_block(idx)
    gate2 = jnp.repeat(per_block(gate), 2, axis=-1).reshape(n // tb, 1, 2 * PEER_SEL * tb)
    chunk = _peer_chunk(n)
    outs = []
    for c0 in range(0, n, chunk):
        b0, b1 = c0 // tb, (c0 + chunk) // tb
        flat = idx_b[b0:b1].reshape(-1)
        gu = sc_gather(wts["peer_u"], flat).reshape(b1 - b0, PEER_SEL * tb, HALF_D)
        gv = sc_gather(wts["peer_v"], flat).reshape(b1 - b0, PEER_SEL * tb, HALF_D)
        outs.append(peer_experts(h1[c0:c0 + chunk], wts["ffn_norm"], gu, gv, gate2[b0:b1], tb))
    h2 = jnp.concatenate(outs, axis=0) if len(outs) > 1 else outs[0]
    y = ple_final(h2, p.reshape(n, -1), wts["ple_proj"], wts["ple_norm"], wts["ple_gate"], wts["final_norm"])
    return y.reshape(B, T, D)


def _prep_weights(attn_norm, w_in, conv_w, a_log_fwd, a_log_bwd, dt_bias_fwd, dt_bias_bwd, dn_out_norm, q_norm, k_norm, w_out, ffn_norm, peer_query, peer_keys_a, peer_keys_b, peer_u, peer_v, ple_proj, ple_norm, ple_gate, final_norm):
    c1 = 4 * DN_WIDTH
    n_gate = 4 * DN_HEADS
    w_pad = jnp.concatenate([w_in[:, :c1], w_in[:, c1:c1 + n_gate], jnp.zeros((D_MODEL, GATE_PAD - n_gate), w_in.dtype),
                             w_in[:, c1 + n_gate:]], axis=1).astype(BF16)
    conv_pad = jnp.concatenate([conv_w, jnp.zeros((SUBLANES - CONV_K, conv_w.shape[1]), conv_w.dtype)], axis=0).astype(F32)
    alog, dtb = _gate_rows(a_log_fwd, a_log_bwd, dt_bias_fwd, dt_bias_bwd)
    hid = np.arange(ATT_WIDTH) // ATT_HEAD_DIM
    head_avg = jnp.asarray((hid[:, None] == hid[None, :]).astype(np.float32) / ATT_HEAD_DIM)
    row = lambda g: g.astype(F32)[None, :]
    return dict(
        attn_norm=row(attn_norm), w_in=w_pad, conv_w=conv_pad, alog=alog, dtb=dtb, dn_out_norm=row(dn_out_norm),
        q_norm=jnp.tile(row(q_norm), (1, ATT_HEADS)), k_norm=jnp.tile(row(k_norm), (1, ATT_KV_HEADS)), head_avg=head_avg,
        w_out=w_out.astype(BF16), ffn_norm=row(ffn_norm), peer_query=peer_query.astype(BF16),
        keys_a=peer_keys_a.astype(BF16), keys_b=peer_keys_b.astype(BF16),
        peer_u=pack_rows(peer_u), peer_v=pack_rows(peer_v),
        ple_proj=ple_proj.astype(BF16), ple_norm=row(ple_norm), ple_gate=ple_gate.astype(BF16), final_norm=row(final_norm))


PEER_TOKENS = 16


def kernel(x_prompt, x_sample, p_prompt, p_sample, attn_norm, w_in, conv_w, a_log_fwd, a_log_bwd, dt_bias_fwd, dt_bias_bwd, dn_out_norm, q_norm, k_norm, w_out, ffn_norm, peer_query, peer_keys_a, peer_keys_b, peer_u, peer_v, ple_proj, ple_norm, ple_gate, final_norm):
    assert attn_norm.shape[0] == 1
    wts = _prep_weights(attn_norm[0], w_in[0], conv_w[0], a_log_fwd[0], a_log_bwd[0], dt_bias_fwd[0], dt_bias_bwd[0],
                        dn_out_norm[0], q_norm[0], k_norm[0], w_out[0], ffn_norm[0], peer_query[0], peer_keys_a[0],
                        peer_keys_b[0], peer_u[0], peer_v[0], ple_proj[0], ple_norm[0], ple_gate[0], final_norm)
    y_p = _layer(x_prompt, p_prompt[0], wts, PEER_TOKENS)
    y_s = _layer(x_sample, p_sample[0], wts, PEER_TOKENS)
    return y_p, y_s
```

```python
import functools
import math

import jax
import jax.numpy as jnp
import numpy as np
from jax import lax
from jax.experimental import pallas as pl
from jax.experimental.pallas import tpu as pltpu
from jax.experimental.pallas import tpu_sc as plsc

F32 = jnp.float32
BF16 = jnp.bfloat16
HI = lax.Precision.HIGHEST

D_MODEL = 1024
GRID_W = 64
PLE_DIM = 256
NORM_EPS = 1e-6
DN_HEADS = 4
DN_HEAD_DIM = 128
DN_WIDTH = DN_HEADS * DN_HEAD_DIM
CONV_K = 5
CHUNK = 64
ATT_HEADS = 8
ATT_KV_HEADS = 2
ATT_GROUP = ATT_HEADS // ATT_KV_HEADS
ATT_HEAD_DIM = 64
ATT_WIDTH = ATT_HEADS * ATT_HEAD_DIM
ATT_KV_WIDTH = ATT_KV_HEADS * ATT_HEAD_DIM
ROPE_THETA = 10000.0
MIX_WIDTH = DN_WIDTH + ATT_WIDTH
N_KEYS = 128
PEER_HEADS = 8
PEER_TOPK = 16
D_KEY_HALF = 128
PEER_SEL = PEER_HEADS * PEER_TOPK
HALF_D = D_MODEL // 2
ATT_KV_TILE = 512

LANES = 128
SUBLANES = 8
SC_CORES = 2
SC_SUBCORES = 16
SC_WORKERS = SC_CORES * SC_SUBCORES
SC_ROWS = 64
VMEM_LIMIT = 56 * 1024 * 1024

GATE_PAD = LANES
IN_PAD_COLS = 3 * DN_WIDTH + DN_WIDTH + GATE_PAD + ATT_WIDTH + 2 * ATT_KV_WIDTH


def _params(sem):
    return pltpu.CompilerParams(dimension_semantics=sem, vmem_limit_bytes=VMEM_LIMIT)


def _rms(x, gain):
    return x * lax.rsqrt(jnp.mean(x * x, axis=-1, keepdims=True) + NORM_EPS) * gain


def _row_tile(n, want):
    t = min(n, want)
    assert n % t == 0
    return t


def _in_proj_kernel(x_ref, g_ref, w_ref, qkv_ref, z_ref, gates_ref, att_ref):
    a = _rms(x_ref[...], g_ref[...]).astype(BF16)
    y = jnp.dot(a, w_ref[...], preferred_element_type=F32)
    c0 = 3 * DN_WIDTH
    c1 = c0 + DN_WIDTH
    c2 = c1 + GATE_PAD
    qkv_ref[...] = y[:, :c0]
    z_ref[...] = y[:, c0:c1]
    gates_ref[...] = y[:, c1:c2]
    att_ref[...] = y[:, c2:]


def in_proj(x, gain, w_pad):
    n = x.shape[0]
    tm = _row_tile(n, 512)
    att_w = ATT_WIDTH + 2 * ATT_KV_WIDTH
    row = lambda w: pl.BlockSpec((tm, w), lambda i: (i, 0))
    return pl.pallas_call(
        _in_proj_kernel,
        grid=(n // tm,),
        in_specs=[row(D_MODEL), pl.BlockSpec((1, D_MODEL), lambda i: (0, 0)),
                  pl.BlockSpec((D_MODEL, IN_PAD_COLS), lambda i: (0, 0))],
        out_specs=[row(3 * DN_WIDTH), row(DN_WIDTH), row(GATE_PAD), row(att_w)],
        out_shape=[jax.ShapeDtypeStruct((n, 3 * DN_WIDTH), F32), jax.ShapeDtypeStruct((n, DN_WIDTH), F32),
                   jax.ShapeDtypeStruct((n, GATE_PAD), F32), jax.ShapeDtypeStruct((n, att_w), F32)],
        compiler_params=_params(("parallel",)),
        name="in_proj",
    )(x, gain, w_pad)


def _conv_kernel(prev_ref, cur_ref, next_ref, w_ref, o_ref, *, tt, n_t):
    t = pl.program_id(1)
    c = pl.program_id(2)
    cur = cur_ref[0]
    prev = jnp.where(t > 0, prev_ref[0], 0.0)
    nxt = jnp.where(t < n_t - 1, next_ref[0], 0.0)
    xe = jnp.concatenate([prev, cur, nxt], axis=0)
    w = w_ref[...]
    pad = CONV_K // 2
    acc = w[0:1] * xe[SUBLANES - pad:SUBLANES - pad + tt]
    for j in range(1, CONV_K):
        lo = SUBLANES - pad + j
        acc = acc + w[j:j + 1] * xe[lo:lo + tt]
    y = acc * jax.nn.sigmoid(acc)
    nrm = y * lax.rsqrt(jnp.sum(y * y, axis=-1, keepdims=True) + NORM_EPS)
    nrm = nrm * jnp.where(c < DN_HEADS, DN_HEAD_DIM ** -0.5, 1.0)
    o_ref[0] = jnp.where(c < 2 * DN_HEADS, nrm, y)


def dn_conv(qkv, conv_w_pad):
    B, T, C = qkv.shape
    tt = _row_tile(T, 1024)
    n_t = T // tt
    r8 = tt // SUBLANES
    last8 = T // SUBLANES - 1
    return pl.pallas_call(
        functools.partial(_conv_kernel, tt=tt, n_t=n_t),
        grid=(B, n_t, C // LANES),
        in_specs=[
            pl.BlockSpec((1, SUBLANES, LANES), lambda b, t, c: (b, jnp.maximum(t * r8 - 1, 0), c)),
            pl.BlockSpec((1, tt, LANES), lambda b, t, c: (b, t, c)),
            pl.BlockSpec((1, SUBLANES, LANES), lambda b, t, c: (b, jnp.minimum((t + 1) * r8, last8), c)),
            pl.BlockSpec((SUBLANES, LANES), lambda b, t, c: (0, c)),
        ],
        out_specs=pl.BlockSpec((1, tt, LANES), lambda b, t, c: (b, t, c)),
        out_shape=jax.ShapeDtypeStruct((B, T, C), F32),
        compiler_params=_params(("parallel", "parallel", "parallel")),
        name="dn_conv",
    )(qkv, qkv, qkv, conv_w_pad)


DELTA_GROUP = 4


def _split3(x):
    a = x.astype(BF16)
    r = x - a.astype(F32)
    b = r.astype(BF16)
    return a, b, (r - b.astype(F32)).astype(BF16)


def _bdot(a, b, dims=(((1,), (0,)), ((), ()))):
    return lax.dot_general(a.astype(BF16), b.astype(BF16), dims, preferred_element_type=F32)


def _delta_kernel(qkv_ref, gates_ref, alog_ref, dtb_ref, o_ref,
                  s_ref, gc_ref, gct_ref, beta_ref, u_ref, w_ref, a_ref, qg_ref, kd_ref, el_ref, *, tb, reverse):
    n_chunks = tb // CHUNK
    nt = (((1,), (1,)), ((), ()))
    tn = (((0,), (0,)), ((), ()))

    @pl.when(pl.program_id(1) == 0)
    def _():
        s_ref[...] = jnp.zeros_like(s_ref)

    gates = gates_ref[0]
    x = gates + dtb_ref[...]
    softplus = jnp.maximum(x, 0.0) + jnp.log1p(jnp.exp(-jnp.abs(x)))
    g = -jnp.exp(alog_ref[...]) * softplus
    beta_ref[...] = jax.nn.sigmoid(gates)
    r = lax.broadcasted_iota(jnp.int32, (tb, tb), 0)
    c = lax.broadcasted_iota(jnp.int32, (tb, tb), 1)
    same = (r // CHUNK) == (c // CHUNK)
    tri = (c >= r) if reverse else (c <= r)
    cum = jnp.where(same & tri, 1.0, 0.0).astype(BF16)
    g3 = _split3(g)
    gc_ref[...] = sum(jnp.dot(cum, p, preferred_element_type=F32) for p in g3)
    gct = sum(lax.dot_general(p, cum, (((0,), (1,)), ((), ())), preferred_element_type=F32) for p in g3)
    for ch in range(n_chunks):
        gct_ref[ch] = gct[:, ch * CHUNK:(ch + 1) * CHUNK]

    ri = lax.broadcasted_iota(jnp.int32, (CHUNK, CHUNK), 0)
    ci = lax.broadcasted_iota(jnp.int32, (CHUNK, CHUNK), 1)
    incl = (ci >= ri) if reverse else (ci <= ri)
    strict = (ci > ri) if reverse else (ci < ri)
    ident = jnp.where(ri == ci, 1.0, 0.0).astype(F32)
    beta_lane0 = DN_HEADS if reverse else 0
    g_lane0 = 3 * DN_HEADS if reverse else 2 * DN_HEADS
    last = 0 if reverse else CHUNK - 1

    def intra_chunk(i, carry):
        chains = []
        for cc in range(DELTA_GROUP):
            ch = i * DELTA_GROUP + cc
            r0 = pl.multiple_of(ch * CHUNK, CHUNK)
            gc_all = gc_ref[pl.ds(r0, CHUNK), :]
            beta_all = beta_ref[pl.ds(r0, CHUNK), :]
            for h in range(DN_HEADS):
                hs = slice(h * DN_HEAD_DIM, (h + 1) * DN_HEAD_DIM)
                q = qkv_ref[0, pl.ds(r0, CHUNK), h * DN_HEAD_DIM:(h + 1) * DN_HEAD_DIM]
                k = qkv_ref[0, pl.ds(r0, CHUNK), (DN_HEADS + h) * DN_HEAD_DIM:(DN_HEADS + h + 1) * DN_HEAD_DIM]
                v = qkv_ref[0, pl.ds(r0, CHUNK), (2 * DN_HEADS + h) * DN_HEAD_DIM:(2 * DN_HEADS + h + 1) * DN_HEAD_DIM]
                gc = gc_all[:, g_lane0 + h:g_lane0 + h + 1]
                beta = beta_all[:, beta_lane0 + h:beta_lane0 + h + 1]
                gc_row = gct_ref[ch, g_lane0 + h:g_lane0 + h + 1, :]
                decay = jnp.where(incl, jnp.exp(jnp.where(incl, gc - gc_row, 0.0)), 0.0)
                kb = k * beta
                eg = jnp.exp(gc)
                g_last = gc[last:last + 1]
                qg_ref[pl.ds(r0, CHUNK), hs] = (q * eg).astype(BF16)
                kd_ref[pl.ds(r0, CHUNK), hs] = (k * jnp.exp(g_last - gc)).astype(BF16)
                el_ref[pl.ds(pl.multiple_of(ch * SUBLANES, SUBLANES), SUBLANES), hs] = jnp.broadcast_to(
                    jnp.exp(g_last), (SUBLANES, DN_HEAD_DIM))
                chains.append(dict(r0=r0, h=h, hs=hs, decay=decay, q=q.astype(BF16), k=k.astype(BF16), kb=kb.astype(BF16),
                                   rhs=jnp.concatenate([v * beta, kb * eg], axis=-1).astype(BF16)))
        for c in chains:
            c["neg"] = -jnp.where(strict, _bdot(c["kb"], c["k"], nt) * c["decay"], 0.0)
        for c in chains:
            a_ref[pl.ds(c["r0"], CHUNK), c["h"] * CHUNK:(c["h"] + 1) * CHUNK] = jnp.where(
                incl, _bdot(c["q"], c["k"], nt) * c["decay"], 0.0).astype(BF16)
        for c in chains:
            c["inv"] = ident + c["neg"]
            c["pw"] = _bdot(c["neg"], c["neg"])
        for _ in range(int(math.log2(CHUNK)) - 2):
            for c in chains:
                c["inv"] = c["inv"] + _bdot(c["inv"], c["pw"])
            for c in chains:
                c["pw"] = _bdot(c["pw"], c["pw"])
        for c in chains:
            c["inv"] = c["inv"] + _bdot(c["inv"], c["pw"])
        for c in chains:
            sol = _bdot(c["inv"], c["rhs"])
            u_ref[pl.ds(c["r0"], CHUNK), c["hs"]] = sol[:, :DN_HEAD_DIM]
            w_ref[pl.ds(c["r0"], CHUNK), c["hs"]] = sol[:, DN_HEAD_DIM:].astype(BF16)
        return carry

    lax.fori_loop(0, n_chunks // DELTA_GROUP, intra_chunk, 0)

    def state_step(i, carry):
        ch = (n_chunks - 1 - i) if reverse else i
        r0 = pl.multiple_of(ch * CHUNK, CHUNK)
        e0 = pl.multiple_of(ch * SUBLANES, SUBLANES)
        heads = [slice(h * DN_HEAD_DIM, (h + 1) * DN_HEAD_DIM) for h in range(DN_HEADS)]
        s = [s_ref[h] for h in range(DN_HEADS)]
        sb = [x.astype(BF16) for x in s]
        ws = [jnp.dot(w_ref[pl.ds(r0, CHUNK), hs], sb[h], preferred_element_type=F32) for h, hs in enumerate(heads)]
        qs = [jnp.dot(qg_ref[pl.ds(r0, CHUNK), hs], sb[h], preferred_element_type=F32) for h, hs in enumerate(heads)]
        vb = [(u_ref[pl.ds(r0, CHUNK), hs] - ws[h]).astype(BF16) for h, hs in enumerate(heads)]
        for h, hs in enumerate(heads):
            s_ref[h] = s[h] * el_ref[pl.ds(e0, 1), hs] + lax.dot_general(
                kd_ref[pl.ds(r0, CHUNK), hs], vb[h], tn, preferred_element_type=F32)
        for h, hs in enumerate(heads):
            o_ref[0, pl.ds(r0, CHUNK), hs] = qs[h] + jnp.dot(
                a_ref[pl.ds(r0, CHUNK), h * CHUNK:(h + 1) * CHUNK], vb[h], preferred_element_type=F32)
        return carry

    lax.fori_loop(0, n_chunks, state_step, 0)


def delta_scan(qkvc, gates, alog_row, dtb_row, reverse):
    B, T, C = qkvc.shape
    tb = _row_tile(T, 512)
    n_b = T // tb
    n_chunks = tb // CHUNK
    assert n_chunks % DELTA_GROUP == 0
    blk = (lambda b, i: (b, n_b - 1 - i, 0)) if reverse else (lambda b, i: (b, i, 0))
    return pl.pallas_call(
        functools.partial(_delta_kernel, tb=tb, reverse=reverse),
        grid=(B, n_b),
        in_specs=[pl.BlockSpec((1, tb, C), blk), pl.BlockSpec((1, tb, GATE_PAD), blk),
                  pl.BlockSpec((1, GATE_PAD), lambda b, i: (0, 0)), pl.BlockSpec((1, GATE_PAD), lambda b, i: (0, 0))],
        out_specs=pl.BlockSpec((1, tb, DN_WIDTH), blk),
        out_shape=jax.ShapeDtypeStruct((B, T, DN_WIDTH), F32),
        scratch_shapes=[pltpu.VMEM((DN_HEADS, DN_HEAD_DIM, DN_HEAD_DIM), F32),
                        pltpu.VMEM((tb, GATE_PAD), F32), pltpu.VMEM((n_chunks, GATE_PAD, CHUNK), F32),
                        pltpu.VMEM((tb, GATE_PAD), F32),
                        pltpu.VMEM((tb, DN_WIDTH), F32), pltpu.VMEM((tb, DN_WIDTH), BF16),
                        pltpu.VMEM((tb, DN_HEADS * CHUNK), BF16),
                        pltpu.VMEM((tb, DN_WIDTH), BF16), pltpu.VMEM((tb, DN_WIDTH), BF16),
                        pltpu.VMEM((n_chunks * SUBLANES, DN_WIDTH), F32)],
        compiler_params=_params(("parallel", "arbitrary")),
        name="delta_bwd" if reverse else "delta_fwd",
    )(qkvc, gates, alog_row, dtb_row)


def _rope(x, cos, sin_signed, first_half):
    partner = jnp.where(first_half, pltpu.roll(x, x.shape[-1] - ATT_HEAD_DIM // 4, axis=1),
                        pltpu.roll(x, ATT_HEAD_DIM // 4, axis=1))
    return x * cos + partner * sin_signed


def _attn_prep_kernel(att_ref, qn_ref, kn_ref, cos_ref, sin_ref, bd_ref, q_ref, k_ref, v_ref):
    att = att_ref[0]
    q = att[:, :ATT_WIDTH]
    k = att[:, ATT_WIDTH:ATT_WIDTH + ATT_KV_WIDTH]
    v = att[:, ATT_WIDTH + ATT_KV_WIDTH:]
    bd = bd_ref[...]
    q_ms = jnp.dot(q * q, bd, preferred_element_type=F32, precision=HI)
    k_ms = jnp.dot(k * k, bd[:ATT_KV_WIDTH, :ATT_KV_WIDTH], preferred_element_type=F32, precision=HI)
    qn = q * lax.rsqrt(q_ms + NORM_EPS) * qn_ref[...]
    kn = k * lax.rsqrt(k_ms + NORM_EPS) * kn_ref[...]
    cos = cos_ref[...]
    sin = sin_ref[...]
    lane = lax.broadcasted_iota(jnp.int32, cos.shape, 1)
    first_half = (lane % (ATT_HEAD_DIM // 2)) < (ATT_HEAD_DIM // 4)
    kr = _rope(kn, cos, sin, first_half)
    vt = v.T
    for h in range(ATT_KV_HEADS):
        k_ref[0, h, 0] = kr[:, h * ATT_HEAD_DIM:(h + 1) * ATT_HEAD_DIM].astype(BF16)
        v_ref[0, h, 0] = vt[h * ATT_HEAD_DIM:(h + 1) * ATT_HEAD_DIM].astype(BF16)
    scale = ATT_HEAD_DIM ** -0.5 * math.log2(math.e)
    for j in range(ATT_WIDTH // LANES):
        qr = _rope(qn[:, j * LANES:(j + 1) * LANES], cos, sin, first_half) * scale
        for e in range(LANES // ATT_HEAD_DIM):
            h = j * (LANES // ATT_HEAD_DIM) + e
            q_ref[0, h] = qr[:, e * ATT_HEAD_DIM:(e + 1) * ATT_HEAD_DIM].astype(BF16)


def attn_prep(att, q_gain, k_gain, cos_t, sin_t, bd):
    B, T, W = att.shape
    tt = _row_tile(T, ATT_KV_TILE)
    n_t = T // tt
    const = lambda shape: pl.BlockSpec(shape, lambda b, t: (0, 0))
    return pl.pallas_call(
        _attn_prep_kernel,
        grid=(B, n_t),
        in_specs=[pl.BlockSpec((1, tt, W), lambda b, t: (b, t, 0)), const((1, ATT_WIDTH)), const((1, ATT_KV_WIDTH)),
                  pl.BlockSpec((tt, LANES), lambda b, t: (t, 0)), pl.BlockSpec((tt, LANES), lambda b, t: (t, 0)),
                  const((ATT_WIDTH, ATT_WIDTH))],
        out_specs=[pl.BlockSpec((1, ATT_HEADS, tt, ATT_HEAD_DIM), lambda b, t: (b, 0, t, 0)),
                   pl.BlockSpec((1, ATT_KV_HEADS, 1, tt, ATT_HEAD_DIM), lambda b, t: (b, 0, t, 0, 0)),
                   pl.BlockSpec((1, ATT_KV_HEADS, 1, ATT_HEAD_DIM, tt), lambda b, t: (b, 0, t, 0, 0))],
        out_shape=[jax.ShapeDtypeStruct((B, ATT_HEADS, T, ATT_HEAD_DIM), BF16),
                   jax.ShapeDtypeStruct((B, ATT_KV_HEADS, n_t, tt, ATT_HEAD_DIM), BF16),
                   jax.ShapeDtypeStruct((B, ATT_KV_HEADS, n_t, ATT_HEAD_DIM, tt), BF16)],
        compiler_params=_params(("parallel", "parallel")),
        name="attn_prep",
    )(att, q_gain, k_gain, cos_t, sin_t, bd)


def _flash_kernel(q_ref, k_ref, vt_ref, o_ref, m_ref, l_ref, acc_ref, st0_ref, st1_ref, *, tq, n_kv):
    m_ref[...] = jnp.full_like(m_ref, -jnp.inf)
    l_ref[...] = jnp.zeros_like(l_ref)
    acc_ref[...] = jnp.zeros_like(acc_ref)
    cols = [slice(g * tq, (g + 1) * tq) for g in range(ATT_GROUP)]

    def score(st_ref, j, g):
        st_ref[:, cols[g]] = lax.dot_general(k_ref[0, 0, j], q_ref[0, g], (((1,), (1,)), ((), ())),
                                             preferred_element_type=F32)

    def consume(st_ref, j, g):
        c = cols[g]
        st = st_ref[:, c]
        m_prev = m_ref[:, c]
        m_new = jnp.maximum(m_prev, jnp.max(st, axis=0, keepdims=True))
        alpha = jnp.exp2(m_prev - m_new)
        p = jnp.exp2(st - m_new)
        l_ref[:, c] = alpha * l_ref[:, c] + jnp.sum(p, axis=0, keepdims=True)
        acc_ref[:, c] = alpha * acc_ref[:, c] + jnp.dot(vt_ref[0, 0, j], p.astype(BF16), preferred_element_type=F32)
        m_ref[:, c] = m_new

    for g in range(ATT_GROUP):
        score(st0_ref, 0, g)

    bufs = (st0_ref, st1_ref)
    unroll = 4 if n_kv % 4 == 0 else 2

    def kv_steps(i, carry):
        for u in range(unroll):
            j = unroll * i + u
            nxt = j + 1 if u < unroll - 1 else jnp.minimum(j + 1, n_kv - 1)
            for g in range(ATT_GROUP):
                score(bufs[(u + 1) % 2], nxt, g)
                consume(bufs[u % 2], j, g)
        return carry

    lax.fori_loop(0, n_kv // unroll, kv_steps, 0)
    if n_kv % 2:
        for g in range(ATT_GROUP):
            consume(st0_ref, n_kv - 1, g)
    o = (acc_ref[...] / l_ref[...]).T
    o_ref[0] = jnp.concatenate([o[c] for c in cols], axis=-1)


def flash_attention(q, k, vt):
    B, _, T, _ = q.shape
    n_kv, tk = k.shape[2], k.shape[3]
    tq = _row_tile(T, 256)
    rows = ATT_GROUP * tq
    return pl.pallas_call(
        functools.partial(_flash_kernel, tq=tq, n_kv=n_kv),
        grid=(B, ATT_KV_HEADS, T // tq),
        in_specs=[pl.BlockSpec((1, ATT_GROUP, tq, ATT_HEAD_DIM), lambda b, h, i: (b, h, i, 0)),
                  pl.BlockSpec((1, 1, n_kv, tk, ATT_HEAD_DIM), lambda b, h, i: (b, h, 0, 0, 0)),
                  pl.BlockSpec((1, 1, n_kv, ATT_HEAD_DIM, tk), lambda b, h, i: (b, h, 0, 0, 0))],
        out_specs=pl.BlockSpec((1, tq, ATT_GROUP * ATT_HEAD_DIM), lambda b, h, i: (b, i, h)),
        out_shape=jax.ShapeDtypeStruct((B, T, ATT_WIDTH), F32),
        scratch_shapes=[pltpu.VMEM((1, rows), F32), pltpu.VMEM((1, rows), F32), pltpu.VMEM((ATT_HEAD_DIM, rows), F32),
                        pltpu.VMEM((tk, rows), F32), pltpu.VMEM((tk, rows), F32)],
        compiler_params=_params(("parallel", "parallel", "arbitrary")),
        name="flash_attention",
    )(q, k, vt)


def _out_proj_kernel(x_ref, of_ref, ob_ref, z_ref, at_ref, gn_ref, w_ref, h_ref):
    o = of_ref[...] + ob_ref[...]
    z = z_ref[...]
    gn = gn_ref[...]
    parts = []
    for h in range(DN_HEADS):
        sl = slice(h * DN_HEAD_DIM, (h + 1) * DN_HEAD_DIM)
        zh = z[:, sl]
        parts.append(_rms(o[:, sl], gn) * (zh * jax.nn.sigmoid(zh)))
    parts.append(at_ref[...])
    mix = jnp.concatenate(parts, axis=-1).astype(BF16)
    h_ref[...] = x_ref[...] + jnp.dot(mix, w_ref[...], preferred_element_type=F32)


def out_proj(x, o_f, o_b, z, at, gn, w_out):
    n = x.shape[0]
    tm = _row_tile(n, 512)
    row = lambda w: pl.BlockSpec((tm, w), lambda i: (i, 0))
    return pl.pallas_call(
        _out_proj_kernel,
        grid=(n // tm,),
        in_specs=[row(D_MODEL), row(DN_WIDTH), row(DN_WIDTH), row(DN_WIDTH), row(ATT_WIDTH),
                  pl.BlockSpec((1, DN_HEAD_DIM), lambda i: (0, 0)), pl.BlockSpec((MIX_WIDTH, D_MODEL), lambda i: (0, 0))],
        out_specs=row(D_MODEL),
        out_shape=jax.ShapeDtypeStruct((n, D_MODEL), F32),
        compiler_params=_params(("parallel",)),
        name="out_proj",
    )(x, o_f, o_b, z, at, gn, w_out)


def _top_rows(s, ids, k):
    rows = lax.broadcasted_iota(jnp.int32, s.shape, 0)
    big = jnp.int32(s.shape[0])
    vals, outs = [], []
    for _ in range(k):
        m = jnp.max(s, axis=0, keepdims=True)
        pos = jnp.min(jnp.where(s == m, rows, big), axis=0, keepdims=True)
        hit = rows == pos
        vals.append(m)
        if ids is None:
            outs.append(pos)
        else:
            outs.append(jnp.max(jnp.where(hit, ids, -1), axis=0, keepdims=True))
        s = jnp.where(hit, -jnp.inf, s)
    return jnp.concatenate(vals, axis=0), jnp.concatenate(outs, axis=0)


def _stair_counts():
    return [PEER_TOPK // (i + 1) for i in range(PEER_TOPK)]


def _route_kernel(h_ref, g_ref, wq_ref, ka_ref, kb_ref, idx_ref, gate_ref):
    xn = _rms(h_ref[...], g_ref[...]).astype(BF16)
    q = jnp.dot(xn, wq_ref[...], preferred_element_type=F32).astype(BF16)
    nt = (((1,), (1,)), ((), ()))
    counts = _stair_counts()
    n_cand = sum(counts)
    pad = (-n_cand) % SUBLANES
    tt = q.shape[0]
    for h in range(PEER_HEADS):
        qa = q[:, (2 * h) * D_KEY_HALF:(2 * h + 1) * D_KEY_HALF]
        qb = q[:, (2 * h + 1) * D_KEY_HALF:(2 * h + 2) * D_KEY_HALF]
        sa = lax.dot_general(ka_ref[...], qa, nt, preferred_element_type=F32)
        sb = lax.dot_general(kb_ref[...], qb, nt, preferred_element_type=F32)
        va, ia = _top_rows(sa, None, PEER_TOPK)
        vb, ib = _top_rows(sb, None, PEER_TOPK)
        cs, ci = [], []
        for i, cnt in enumerate(counts):
            cs.append(va[i:i + 1] + vb[:cnt])
            ci.append(ia[i:i + 1] * N_KEYS + ib[:cnt])
        if pad:
            cs.append(jnp.full((pad, tt), -jnp.inf, F32))
            ci.append(jnp.zeros((pad, tt), jnp.int32))
        top_s, top_i = _top_rows(jnp.concatenate(cs, axis=0), jnp.concatenate(ci, axis=0), PEER_TOPK)
        e = jnp.exp(top_s - top_s[0:1])
        gate_ref[0, h * PEER_TOPK:(h + 1) * PEER_TOPK, :] = e / jnp.sum(e, axis=0, keepdims=True)
        idx_ref[0, h * PEER_TOPK:(h + 1) * PEER_TOPK, :] = top_i


def peer_route(h1, gain, wq, keys_a, keys_b):
    n = h1.shape[0]
    tt = _row_tile(n, 256)
    const = lambda shape: pl.BlockSpec(shape, lambda i: (0, 0))
    out = pl.BlockSpec((1, PEER_SEL, tt), lambda i: (i, 0, 0))
    return pl.pallas_call(
        _route_kernel,
        grid=(n // tt,),
        in_specs=[pl.BlockSpec((tt, D_MODEL), lambda i: (i, 0)), const((1, D_MODEL)),
                  const((D_MODEL, PEER_HEADS * 2 * D_KEY_HALF)), const((N_KEYS, D_KEY_HALF)), const((N_KEYS, D_KEY_HALF))],
        out_specs=[out, out],
        out_shape=[jax.ShapeDtypeStruct((n // tt, PEER_SEL, tt), jnp.int32),
                   jax.ShapeDtypeStruct((n // tt, PEER_SEL, tt), F32)],
        compiler_params=_params(("parallel",)),
        name="peer_route",
    )(h1, gain, wq, keys_a, keys_b)


def pack_rows(t):
    b = lax.bitcast_convert_type(t.astype(BF16), jnp.uint16).astype(jnp.uint32)
    half = t.shape[1] // 2
    return b[:, :half] | (b[:, half:] << 16)


def sc_gather(table, idx):
    n_idx = idx.shape[0]
    _, width = table.shape
    per_w = n_idx // SC_WORKERS
    n_it = per_w // (2 * SC_ROWS)
    assert per_w * SC_WORKERS == n_idx and n_it * 2 * SC_ROWS == per_w and n_it >= 1
    mesh = plsc.VectorSubcoreMesh(core_axis_name="c", subcore_axis_name="s")

    @functools.partial(
        pl.kernel, mesh=mesh,
        out_type=jax.ShapeDtypeStruct((n_idx, width), table.dtype),
        scratch_types=[pltpu.VMEM((per_w,), jnp.int32),
                       pltpu.VMEM((SC_ROWS, width), table.dtype), pltpu.VMEM((SC_ROWS, width), table.dtype),
                       pltpu.SemaphoreType.DMA, pltpu.SemaphoreType.DMA,
                       pltpu.SemaphoreType.DMA, pltpu.SemaphoreType.DMA],
        name="peer_gather",
    )
    def k(table_hbm, idx_hbm, out_hbm, idx_v, buf0, buf1, g0, g1, s0, s1):
        base = (lax.axis_index("s") * SC_CORES + lax.axis_index("c")) * per_w
        pltpu.sync_copy(idx_hbm.at[pl.ds(base, per_w)], idx_v)
        bufs, gsem, ssem = (buf0, buf1), (g0, g1), (s0, s1)

        def gather(c, b):
            return pltpu.make_async_copy(table_hbm.at[idx_v.at[pl.ds(c * SC_ROWS, SC_ROWS)]], bufs[b], gsem[b])

        def put(c, b):
            return pltpu.make_async_copy(bufs[b], out_hbm.at[pl.ds(base + c * SC_ROWS, SC_ROWS)], ssem[b])

        @pl.loop(0, n_it)
        def _(i):
            for b in range(2):
                @pl.when(i > 0)
                def _():
                    put(2 * i + b - 2, b).wait()

                gather(2 * i + b, b).start()
            for b in range(2):
                gather(2 * i + b, b).wait()
                put(2 * i + b, b).start()

        for b in range(2):
            put(2 * (n_it - 1) + b, b).wait()

    return k(table, idx)


def _expert_kernel(h_ref, g_ref, gu_ref, gv_ref, gate_ref, o_ref, *, tb):
    m2 = 2 * PEER_SEL * tb
    h1 = h_ref[...]
    xn = _rms(h1, g_ref[...]).astype(BF16)
    xs = jnp.concatenate([xn[:, :HALF_D], xn[:, HALF_D:]], axis=0)
    gu = pltpu.bitcast(gu_ref[0], BF16)
    gv = pltpu.bitcast(gv_ref[0], BF16)
    r = lax.dot_general(xs, gu, (((1,), (1,)), ((), ())), preferred_element_type=F32)
    row = lax.broadcasted_iota(jnp.int32, (2 * tb, m2), 0)
    col = lax.broadcasted_iota(jnp.int32, (2 * tb, m2), 1)
    mine = ((col // 2) % tb + (col % 2) * tb) == row
    s = jnp.sum(jnp.where(mine, r, 0.0), axis=0, keepdims=True)
    lane = lax.broadcasted_iota(jnp.int32, (1, m2), 1)
    pair = s + pltpu.roll(s, 1, axis=1)
    hsum = jnp.where(lane % 2 == 1, pair, pltpu.roll(pair, m2 - 1, axis=1))
    gelu = 0.5 * hsum * (1.0 + lax.erf(hsum * math.sqrt(0.5)))
    act = gelu * gate_ref[0]
    wm = jnp.where(mine, jnp.broadcast_to(act, (2 * tb, m2)), 0.0).astype(BF16)
    o = jnp.dot(wm, gv, preferred_element_type=F32)
    o_ref[...] = h1 + jnp.concatenate([o[:tb], o[tb:]], axis=-1)


def peer_experts(h1, gain, gu, gv, gate2, tb):
    n = h1.shape[0]
    m = PEER_SEL * tb
    return pl.pallas_call(
        functools.partial(_expert_kernel, tb=tb),
        grid=(n // tb,),
        in_specs=[pl.BlockSpec((tb, D_MODEL), lambda i: (i, 0)), pl.BlockSpec((1, D_MODEL), lambda i: (0, 0)),
                  pl.BlockSpec((1, m, HALF_D), lambda i: (i, 0, 0)), pl.BlockSpec((1, m, HALF_D), lambda i: (i, 0, 0)),
                  pl.BlockSpec((1, 1, 2 * m), lambda i: (i, 0, 0))],
        out_specs=pl.BlockSpec((tb, D_MODEL), lambda i: (i, 0)),
        out_shape=jax.ShapeDtypeStruct((n, D_MODEL), F32),
        compiler_params=_params(("parallel",)),
        name="peer_experts",
    )(h1, gain, gu, gv, gate2)


def _ple_kernel(h_ref, p_ref, wp_ref, pn_ref, wg_ref, fn_ref, y_ref):
    h = h_ref[...]
    ple = _rms(jnp.dot(p_ref[...].astype(BF16), wp_ref[...], preferred_element_type=F32), pn_ref[...])
    gate = jax.nn.sigmoid(jnp.dot(h.astype(BF16), wg_ref[...], preferred_element_type=F32))
    y_ref[...] = _rms(h + ple * gate, fn_ref[...])


def ple_final(h2, p, wp, pn, wg, fn):
    n = h2.shape[0]
    tm = _row_tile(n, 512)
    row = lambda w: pl.BlockSpec((tm, w), lambda i: (i, 0))
    const = lambda shape: pl.BlockSpec(shape, lambda i: (0, 0))
    return pl.pallas_call(
        _ple_kernel,
        grid=(n // tm,),
        in_specs=[row(D_MODEL), row(PLE_DIM), const((PLE_DIM, D_MODEL)), const((1, D_MODEL)),
                  const((D_MODEL, D_MODEL)), const((1, D_MODEL))],
        out_specs=row(D_MODEL),
        out_shape=jax.ShapeDtypeStruct((n, D_MODEL), F32),
        compiler_params=_params(("parallel",)),
        name="ple_final",
    )(h2, p, wp, pn, wg, fn)


def _rope_tables(T):
    quarter = ATT_HEAD_DIM // 4
    inv_freq = ROPE_THETA ** (-jnp.arange(0, 2 * quarter, 2, dtype=F32) / (2 * quarter))
    t = jnp.arange(T)
    row_pos = (t // GRID_W).astype(F32)
    col_pos = (t % GRID_W).astype(F32)
    ang_r = row_pos[:, None] * inv_freq[None, :]
    ang_c = col_pos[:, None] * inv_freq[None, :]
    ang = jnp.concatenate([ang_r, ang_r, ang_c, ang_c], axis=-1)
    sign = jnp.tile(jnp.concatenate([-jnp.ones((quarter,), F32), jnp.ones((quarter,), F32)]), 2)
    reps = LANES // ATT_HEAD_DIM
    return jnp.tile(jnp.cos(ang), (1, reps)), jnp.tile(jnp.sin(ang) * sign, (1, reps))


def _gate_rows(a_log_f, a_log_b, dtb_f, dtb_b):
    zeros = jnp.zeros((2 * DN_HEADS,), F32)
    tail = jnp.zeros((GATE_PAD - 4 * DN_HEADS,), F32)
    alog = jnp.concatenate([zeros, a_log_f.astype(F32), a_log_b.astype(F32), tail])[None, :]
    dtb = jnp.concatenate([zeros, dtb_f.astype(F32), dtb_b.astype(F32), tail])[None, :]
    return alog, dtb


def _peer_chunk(n):
    for c in (4096, 2048, 1024, 512, 256, 128, 64):
        if n % c == 0:
            return c
    raise ValueError(n)


def _layer_front(x, wts, tb):
    B, T, D = x.shape
    n = B * T
    xf = x.reshape(n, D)
    qkv, z, gates, att = in_proj(xf, wts["attn_norm"], wts["w_in"])
    qkvc = dn_conv(qkv.reshape(B, T, -1), wts["conv_w"])
    gates3 = gates.reshape(B, T, -1)
    o_f = delta_scan(qkvc, gates3, wts["alog"], wts["dtb"], reverse=False)
    o_b = delta_scan(qkvc, gates3, wts["alog"], wts["dtb"], reverse=True)
    cos_t, sin_t = _rope_tables(T)
    q, k, v = attn_prep(att.reshape(B, T, -1), wts["q_norm"], wts["k_norm"], cos_t, sin_t, wts["head_avg"])
    at = flash_attention(q, k, v)
    h1 = out_proj(xf, o_f.reshape(n, -1), o_b.reshape(n, -1), z, at.reshape(n, -1), wts["dn_out_norm"], wts["w_out"])
    idx, gate = peer_route(h1, wts["ffn_norm"], wts["peer_query"], wts["keys_a"], wts["keys_b"])
    tt = idx.shape[-1]

    def per_block(a):
        a = a.reshape(n // tt, PEER_SEL, tt // tb, tb).transpose(0, 2, 1, 3)
        return a.reshape(n // tb, PEER_SEL * tb)

    idx_b = per_block(idx)
    gate2 = jnp.repeat(per_block(gate), 2, axis=-1).reshape(n // tb, 1, 2 * PEER_SEL * tb)
    chunk = _peer_chunk(n)
    rows = []
    for c0 in range(0, n, chunk):
        b0, b1 = c0 // tb, (c0 + chunk) // tb
        flat = idx_b[b0:b1].reshape(-1)
        rows.append((sc_gather(wts["peer_u"], flat).reshape(b1 - b0, PEER_SEL * tb, HALF_D),
                     sc_gather(wts["peer_v"], flat).reshape(b1 - b0, PEER_SEL * tb, HALF_D)))
    return h1, rows, gate2


def _layer_back(h1, rows, gate2, p, wts, tb):
    n = h1.shape[0]
    chunk = n // len(rows)
    outs = []
    for c, (gu, gv) in enumerate(rows):
        b0, b1 = c * chunk // tb, (c + 1) * chunk // tb
        outs.append(peer_experts(h1[c * chunk:(c + 1) * chunk], wts["ffn_norm"], gu, gv, gate2[b0:b1], tb))
    h2 = jnp.concatenate(outs, axis=0) if len(outs) > 1 else outs[0]
    return ple_final(h2, p.reshape(n, -1), wts["ple_proj"], wts["ple_norm"], wts["ple_gate"], wts["final_norm"])


def _layer(x, p, wts, tb):
    h1, rows, gate2 = _layer_front(x, wts, tb)
    return _layer_back(h1, rows, gate2, p, wts, tb).reshape(x.shape)


def _prep_weights(attn_norm, w_in, conv_w, a_log_fwd, a_log_bwd, dt_bias_fwd, dt_bias_bwd, dn_out_norm, q_norm, k_norm, w_out, ffn_norm, peer_query, peer_keys_a, peer_keys_b, peer_u, peer_v, ple_proj, ple_norm, ple_gate, final_norm):
    c1 = 4 * DN_WIDTH
    n_gate = 4 * DN_HEADS
    w_pad = jnp.concatenate([w_in[:, :c1], w_in[:, c1:c1 + n_gate], jnp.zeros((D_MODEL, GATE_PAD - n_gate), w_in.dtype),
                             w_in[:, c1 + n_gate:]], axis=1).astype(BF16)
    conv_pad = jnp.concatenate([conv_w, jnp.zeros((SUBLANES - CONV_K, conv_w.shape[1]), conv_w.dtype)], axis=0).astype(F32)
    alog, dtb = _gate_rows(a_log_fwd, a_log_bwd, dt_bias_fwd, dt_bias_bwd)
    hid = np.arange(ATT_WIDTH) // ATT_HEAD_DIM
    head_avg = jnp.asarray((hid[:, None] == hid[None, :]).astype(np.float32) / ATT_HEAD_DIM)
    row = lambda g: g.astype(F32)[None, :]
    return dict(
        attn_norm=row(attn_norm), w_in=w_pad, conv_w=conv_pad, alog=alog, dtb=dtb, dn_out_norm=row(dn_out_norm),
        q_norm=jnp.tile(row(q_norm), (1, ATT_HEADS)), k_norm=jnp.tile(row(k_norm), (1, ATT_KV_HEADS)), head_avg=head_avg,
        w_out=w_out.astype(BF16), ffn_norm=row(ffn_norm), peer_query=peer_query.astype(BF16),
        keys_a=peer_keys_a.astype(BF16), keys_b=peer_keys_b.astype(BF16),
        peer_u=pack_rows(peer_u), peer_v=pack_rows(peer_v),
        ple_proj=ple_proj.astype(BF16), ple_norm=row(ple_norm), ple_gate=ple_gate.astype(BF16), final_norm=row(final_norm))


PEER_TOKENS = 16


def kernel(x_prompt, x_sample, p_prompt, p_sample, attn_norm, w_in, conv_w, a_log_fwd, a_log_bwd, dt_bias_fwd, dt_bias_bwd, dn_out_norm, q_norm, k_norm, w_out, ffn_norm, peer_query, peer_keys_a, peer_keys_b, peer_u, peer_v, ple_proj, ple_norm, ple_gate, final_norm):
    assert attn_norm.shape[0] == 1
    wts = _prep_weights(attn_norm[0], w_in[0], conv_w[0], a_log_fwd[0], a_log_bwd[0], dt_bias_fwd[0], dt_bias_bwd[0],
                        dn_out_norm[0], q_norm[0], k_norm[0], w_out[0], ffn_norm[0], peer_query[0], peer_keys_a[0],
                        peer_keys_b[0], peer_u[0], peer_v[0], ple_proj[0], ple_norm[0], ple_gate[0], final_norm)
    pieces = [(x_sample[b:b + 1], p_sample[0, b]) for b in range(x_sample.shape[0])]
    pieces += [(x_prompt[b:b + 1], p_prompt[0, b]) for b in range(x_prompt.shape[0])]
    fronts = [_layer_front(x, wts, PEER_TOKENS) for x, _ in pieces]
    ys = [_layer_back(*f, p, wts, PEER_TOKENS) for f, (_, p) in zip(fronts, pieces)]
    n_s = x_sample.shape[0]
    return jnp.stack(ys[n_s:], axis=0), jnp.stack(ys[:n_s], axis=0)
```

```python
import functools
import math

import jax
import jax.numpy as jnp
import numpy as np
from jax import lax
from jax.experimental import pallas as pl
from jax.experimental.pallas import tpu as pltpu
from jax.experimental.pallas import tpu_sc as plsc

F32 = jnp.float32
BF16 = jnp.bfloat16
HI = lax.Precision.HIGHEST

D_MODEL = 1024
GRID_W = 64
PLE_DIM = 256
NORM_EPS = 1e-6
DN_HEADS = 4
DN_HEAD_DIM = 128
DN_WIDTH = DN_HEADS * DN_HEAD_DIM
CONV_K = 5
CHUNK = 64
ATT_HEADS = 8
ATT_KV_HEADS = 2
ATT_GROUP = ATT_HEADS // ATT_KV_HEADS
ATT_HEAD_DIM = 64
ATT_WIDTH = ATT_HEADS * ATT_HEAD_DIM
ATT_KV_WIDTH = ATT_KV_HEADS * ATT_HEAD_DIM
ROPE_THETA = 10000.0
MIX_WIDTH = DN_WIDTH + ATT_WIDTH
N_KEYS = 128
PEER_HEADS = 8
PEER_TOPK = 16
D_KEY_HALF = 128
PEER_SEL = PEER_HEADS * PEER_TOPK
HALF_D = D_MODEL // 2
ATT_KV_TILE = 512

LANES = 128
SUBLANES = 8
SC_CORES = 2
SC_SUBCORES = 16
SC_WORKERS = SC_CORES * SC_SUBCORES
SC_LANES = 16
N_EXPERTS = N_KEYS * N_KEYS
EXP_BLOCKS = N_EXPERTS // LANES
SC_RANGE = 16
PEER_ROW_TILE = 512
PEER_EXPERT_TILE = 2048
VMEM_LIMIT = 56 * 1024 * 1024

GATE_PAD = LANES
IN_PAD_COLS = 3 * DN_WIDTH + DN_WIDTH + GATE_PAD + ATT_WIDTH + 2 * ATT_KV_WIDTH


def _params(sem):
    return pltpu.CompilerParams(dimension_semantics=sem, vmem_limit_bytes=VMEM_LIMIT)


def _rms(x, gain):
    return x * lax.rsqrt(jnp.mean(x * x, axis=-1, keepdims=True) + NORM_EPS) * gain


def _row_tile(n, want):
    t = min(n, want)
    assert n % t == 0
    return t


def _in_proj_kernel(x_ref, g_ref, w_ref, qkv_ref, z_ref, gates_ref, att_ref):
    a = _rms(x_ref[...], g_ref[...]).astype(BF16)
    y = jnp.dot(a, w_ref[...], preferred_element_type=F32)
    c0 = 3 * DN_WIDTH
    c1 = c0 + DN_WIDTH
    c2 = c1 + GATE_PAD
    qkv_ref[...] = y[:, :c0]
    z_ref[...] = y[:, c0:c1]
    gates_ref[...] = y[:, c1:c2]
    att_ref[...] = y[:, c2:]


def in_proj(x, gain, w_pad):
    n = x.shape[0]
    tm = _row_tile(n, 512)
    att_w = ATT_WIDTH + 2 * ATT_KV_WIDTH
    row = lambda w: pl.BlockSpec((tm, w), lambda i: (i, 0))
    return pl.pallas_call(
        _in_proj_kernel,
        grid=(n // tm,),
        in_specs=[row(D_MODEL), pl.BlockSpec((1, D_MODEL), lambda i: (0, 0)),
                  pl.BlockSpec((D_MODEL, IN_PAD_COLS), lambda i: (0, 0))],
        out_specs=[row(3 * DN_WIDTH), row(DN_WIDTH), row(GATE_PAD), row(att_w)],
        out_shape=[jax.ShapeDtypeStruct((n, 3 * DN_WIDTH), F32), jax.ShapeDtypeStruct((n, DN_WIDTH), F32),
                   jax.ShapeDtypeStruct((n, GATE_PAD), F32), jax.ShapeDtypeStruct((n, att_w), F32)],
        compiler_params=_params(("parallel",)),
        name="in_proj",
    )(x, gain, w_pad)


def _conv_kernel(prev_ref, cur_ref, next_ref, w_ref, o_ref, *, tt, n_t):
    t = pl.program_id(1)
    c = pl.program_id(2)
    cur = cur_ref[0]
    prev = jnp.where(t > 0, prev_ref[0], 0.0)
    nxt = jnp.where(t < n_t - 1, next_ref[0], 0.0)
    xe = jnp.concatenate([prev, cur, nxt], axis=0)
    w = w_ref[...]
    pad = CONV_K // 2
    acc = w[0:1] * xe[SUBLANES - pad:SUBLANES - pad + tt]
    for j in range(1, CONV_K):
        lo = SUBLANES - pad + j
        acc = acc + w[j:j + 1] * xe[lo:lo + tt]
    y = acc * jax.nn.sigmoid(acc)
    nrm = y * lax.rsqrt(jnp.sum(y * y, axis=-1, keepdims=True) + NORM_EPS)
    nrm = nrm * jnp.where(c < DN_HEADS, DN_HEAD_DIM ** -0.5, 1.0)
    o_ref[0] = jnp.where(c < 2 * DN_HEADS, nrm, y)


def dn_conv(qkv, conv_w_pad):
    B, T, C = qkv.shape
    tt = _row_tile(T, 1024)
    n_t = T // tt
    r8 = tt // SUBLANES
    last8 = T // SUBLANES - 1
    return pl.pallas_call(
        functools.partial(_conv_kernel, tt=tt, n_t=n_t),
        grid=(B, n_t, C // LANES),
        in_specs=[
            pl.BlockSpec((1, SUBLANES, LANES), lambda b, t, c: (b, jnp.maximum(t * r8 - 1, 0), c)),
            pl.BlockSpec((1, tt, LANES), lambda b, t, c: (b, t, c)),
            pl.BlockSpec((1, SUBLANES, LANES), lambda b, t, c: (b, jnp.minimum((t + 1) * r8, last8), c)),
            pl.BlockSpec((SUBLANES, LANES), lambda b, t, c: (0, c)),
        ],
        out_specs=pl.BlockSpec((1, tt, LANES), lambda b, t, c: (b, t, c)),
        out_shape=jax.ShapeDtypeStruct((B, T, C), F32),
        compiler_params=_params(("parallel", "parallel", "parallel")),
        name="dn_conv",
    )(qkv, qkv, qkv, conv_w_pad)


DELTA_GROUP = 4


def _split3(x):
    a = x.astype(BF16)
    r = x - a.astype(F32)
    b = r.astype(BF16)
    return a, b, (r - b.astype(F32)).astype(BF16)


def _bdot(a, b, dims=(((1,), (0,)), ((), ()))):
    return lax.dot_general(a.astype(BF16), b.astype(BF16), dims, preferred_element_type=F32)


def _delta_kernel(qkv_ref, gates_ref, alog_ref, dtb_ref, o_ref,
                  s_ref, gc_ref, gct_ref, beta_ref, u_ref, w_ref, a_ref, qg_ref, kd_ref, el_ref, *, tb, reverse):
    n_chunks = tb // CHUNK
    nt = (((1,), (1,)), ((), ()))
    tn = (((0,), (0,)), ((), ()))

    @pl.when(pl.program_id(1) == 0)
    def _():
        s_ref[...] = jnp.zeros_like(s_ref)

    gates = gates_ref[0]
    x = gates + dtb_ref[...]
    softplus = jnp.maximum(x, 0.0) + jnp.log1p(jnp.exp(-jnp.abs(x)))
    g = -jnp.exp(alog_ref[...]) * softplus
    beta_ref[...] = jax.nn.sigmoid(gates)
    r = lax.broadcasted_iota(jnp.int32, (tb, tb), 0)
    c = lax.broadcasted_iota(jnp.int32, (tb, tb), 1)
    same = (r // CHUNK) == (c // CHUNK)
    tri = (c >= r) if reverse else (c <= r)
    cum = jnp.where(same & tri, 1.0, 0.0).astype(BF16)
    g3 = _split3(g)
    gc_ref[...] = sum(jnp.dot(cum, p, preferred_element_type=F32) for p in g3)
    gct = sum(lax.dot_general(p, cum, (((0,), (1,)), ((), ())), preferred_element_type=F32) for p in g3)
    for ch in range(n_chunks):
        gct_ref[ch] = gct[:, ch * CHUNK:(ch + 1) * CHUNK]

    ri = lax.broadcasted_iota(jnp.int32, (CHUNK, CHUNK), 0)
    ci = lax.broadcasted_iota(jnp.int32, (CHUNK, CHUNK), 1)
    incl = (ci >= ri) if reverse else (ci <= ri)
    strict = (ci > ri) if reverse else (ci < ri)
    ident = jnp.where(ri == ci, 1.0, 0.0).astype(F32)
    beta_lane0 = DN_HEADS if reverse else 0
    g_lane0 = 3 * DN_HEADS if reverse else 2 * DN_HEADS
    last = 0 if reverse else CHUNK - 1

    def intra_chunk(i, carry):
        chains = []
        for cc in range(DELTA_GROUP):
            ch = i * DELTA_GROUP + cc
            r0 = pl.multiple_of(ch * CHUNK, CHUNK)
            gc_all = gc_ref[pl.ds(r0, CHUNK), :]
            beta_all = beta_ref[pl.ds(r0, CHUNK), :]
            for h in range(DN_HEADS):
                hs = slice(h * DN_HEAD_DIM, (h + 1) * DN_HEAD_DIM)
                q = qkv_ref[0, pl.ds(r0, CHUNK), h * DN_HEAD_DIM:(h + 1) * DN_HEAD_DIM]
                k = qkv_ref[0, pl.ds(r0, CHUNK), (DN_HEADS + h) * DN_HEAD_DIM:(DN_HEADS + h + 1) * DN_HEAD_DIM]
                v = qkv_ref[0, pl.ds(r0, CHUNK), (2 * DN_HEADS + h) * DN_HEAD_DIM:(2 * DN_HEADS + h + 1) * DN_HEAD_DIM]
                gc = gc_all[:, g_lane0 + h:g_lane0 + h + 1]
                beta = beta_all[:, beta_lane0 + h:beta_lane0 + h + 1]
                gc_row = gct_ref[ch, g_lane0 + h:g_lane0 + h + 1, :]
                decay = jnp.where(incl, jnp.exp(jnp.where(incl, gc - gc_row, 0.0)), 0.0)
                kb = k * beta
                eg = jnp.exp(gc)
                g_last = gc[last:last + 1]
                qg_ref[pl.ds(r0, CHUNK), hs] = (q * eg).astype(BF16)
                kd_ref[pl.ds(r0, CHUNK), hs] = (k * jnp.exp(g_last - gc)).astype(BF16)
                el_ref[pl.ds(pl.multiple_of(ch * SUBLANES, SUBLANES), SUBLANES), hs] = jnp.broadcast_to(
                    jnp.exp(g_last), (SUBLANES, DN_HEAD_DIM))
                chains.append(dict(r0=r0, h=h, hs=hs, decay=decay, q=q.astype(BF16), k=k.astype(BF16), kb=kb.astype(BF16),
                                   rhs=jnp.concatenate([v * beta, kb * eg], axis=-1).astype(BF16)))
        for c in chains:
            c["neg"] = -jnp.where(strict, _bdot(c["kb"], c["k"], nt) * c["decay"], 0.0)
        for c in chains:
            a_ref[pl.ds(c["r0"], CHUNK), c["h"] * CHUNK:(c["h"] + 1) * CHUNK] = jnp.where(
                incl, _bdot(c["q"], c["k"], nt) * c["decay"], 0.0).astype(BF16)
        for c in chains:
            c["inv"] = ident + c["neg"]
            c["pw"] = _bdot(c["neg"], c["neg"])
        for _ in range(int(math.log2(CHUNK)) - 2):
            for c in chains:
                c["inv"] = c["inv"] + _bdot(c["inv"], c["pw"])
            for c in chains:
                c["pw"] = _bdot(c["pw"], c["pw"])
        for c in chains:
            c["inv"] = c["inv"] + _bdot(c["inv"], c["pw"])
        for c in chains:
            sol = _bdot(c["inv"], c["rhs"])
            u_ref[pl.ds(c["r0"], CHUNK), c["hs"]] = sol[:, :DN_HEAD_DIM]
            w_ref[pl.ds(c["r0"], CHUNK), c["hs"]] = sol[:, DN_HEAD_DIM:].astype(BF16)
        return carry

    lax.fori_loop(0, n_chunks // DELTA_GROUP, intra_chunk, 0)

    def state_step(i, carry):
        ch = (n_chunks - 1 - i) if reverse else i
        r0 = pl.multiple_of(ch * CHUNK, CHUNK)
        e0 = pl.multiple_of(ch * SUBLANES, SUBLANES)
        heads = [slice(h * DN_HEAD_DIM, (h + 1) * DN_HEAD_DIM) for h in range(DN_HEADS)]
        s = [s_ref[h] for h in range(DN_HEADS)]
        sb = [x.astype(BF16) for x in s]
        ws = [jnp.dot(w_ref[pl.ds(r0, CHUNK), hs], sb[h], preferred_element_type=F32) for h, hs in enumerate(heads)]
        qs = [jnp.dot(qg_ref[pl.ds(r0, CHUNK), hs], sb[h], preferred_element_type=F32) for h, hs in enumerate(heads)]
        vb = [(u_ref[pl.ds(r0, CHUNK), hs] - ws[h]).astype(BF16) for h, hs in enumerate(heads)]
        for h, hs in enumerate(heads):
            s_ref[h] = s[h] * el_ref[pl.ds(e0, 1), hs] + lax.dot_general(
                kd_ref[pl.ds(r0, CHUNK), hs], vb[h], tn, preferred_element_type=F32)
        for h, hs in enumerate(heads):
            o_ref[0, pl.ds(r0, CHUNK), hs] = qs[h] + jnp.dot(
                a_ref[pl.ds(r0, CHUNK), h * CHUNK:(h + 1) * CHUNK], vb[h], preferred_element_type=F32)
        return carry

    lax.fori_loop(0, n_chunks, state_step, 0)


def delta_scan(qkvc, gates, alog_row, dtb_row, reverse):
    B, T, C = qkvc.shape
    tb = _row_tile(T, 512)
    n_b = T // tb
    n_chunks = tb // CHUNK
    assert n_chunks % DELTA_GROUP == 0
    blk = (lambda b, i: (b, n_b - 1 - i, 0)) if reverse else (lambda b, i: (b, i, 0))
    return pl.pallas_call(
        functools.partial(_delta_kernel, tb=tb, reverse=reverse),
        grid=(B, n_b),
        in_specs=[pl.BlockSpec((1, tb, C), blk), pl.BlockSpec((1, tb, GATE_PAD), blk),
                  pl.BlockSpec((1, GATE_PAD), lambda b, i: (0, 0)), pl.BlockSpec((1, GATE_PAD), lambda b, i: (0, 0))],
        out_specs=pl.BlockSpec((1, tb, DN_WIDTH), blk),
        out_shape=jax.ShapeDtypeStruct((B, T, DN_WIDTH), F32),
        scratch_shapes=[pltpu.VMEM((DN_HEADS, DN_HEAD_DIM, DN_HEAD_DIM), F32),
                        pltpu.VMEM((tb, GATE_PAD), F32), pltpu.VMEM((n_chunks, GATE_PAD, CHUNK), F32),
                        pltpu.VMEM((tb, GATE_PAD), F32),
                        pltpu.VMEM((tb, DN_WIDTH), F32), pltpu.VMEM((tb, DN_WIDTH), BF16),
                        pltpu.VMEM((tb, DN_HEADS * CHUNK), BF16),
                        pltpu.VMEM((tb, DN_WIDTH), BF16), pltpu.VMEM((tb, DN_WIDTH), BF16),
                        pltpu.VMEM((n_chunks * SUBLANES, DN_WIDTH), F32)],
        compiler_params=_params(("parallel", "arbitrary")),
        name="delta_bwd" if reverse else "delta_fwd",
    )(qkvc, gates, alog_row, dtb_row)


def _rope(x, cos, sin_signed, first_half):
    partner = jnp.where(first_half, pltpu.roll(x, x.shape[-1] - ATT_HEAD_DIM // 4, axis=1),
                        pltpu.roll(x, ATT_HEAD_DIM // 4, axis=1))
    return x * cos + partner * sin_signed


def _attn_prep_kernel(att_ref, qn_ref, kn_ref, cos_ref, sin_ref, bd_ref, q_ref, k_ref, v_ref):
    att = att_ref[0]
    q = att[:, :ATT_WIDTH]
    k = att[:, ATT_WIDTH:ATT_WIDTH + ATT_KV_WIDTH]
    v = att[:, ATT_WIDTH + ATT_KV_WIDTH:]
    bd = bd_ref[...]
    q_ms = jnp.dot(q * q, bd, preferred_element_type=F32, precision=HI)
    k_ms = jnp.dot(k * k, bd[:ATT_KV_WIDTH, :ATT_KV_WIDTH], preferred_element_type=F32, precision=HI)
    qn = q * lax.rsqrt(q_ms + NORM_EPS) * qn_ref[...]
    kn = k * lax.rsqrt(k_ms + NORM_EPS) * kn_ref[...]
    cos = cos_ref[...]
    sin = sin_ref[...]
    lane = lax.broadcasted_iota(jnp.int32, cos.shape, 1)
    first_half = (lane % (ATT_HEAD_DIM // 2)) < (ATT_HEAD_DIM // 4)
    kr = _rope(kn, cos, sin, first_half)
    vt = v.T
    for h in range(ATT_KV_HEADS):
        k_ref[0, h, 0] = kr[:, h * ATT_HEAD_DIM:(h + 1) * ATT_HEAD_DIM].astype(BF16)
        v_ref[0, h, 0] = vt[h * ATT_HEAD_DIM:(h + 1) * ATT_HEAD_DIM].astype(BF16)
    scale = ATT_HEAD_DIM ** -0.5 * math.log2(math.e)
    for j in range(ATT_WIDTH // LANES):
        qr = _rope(qn[:, j * LANES:(j + 1) * LANES], cos, sin, first_half) * scale
        for e in range(LANES // ATT_HEAD_DIM):
            h = j * (LANES // ATT_HEAD_DIM) + e
            q_ref[0, h] = qr[:, e * ATT_HEAD_DIM:(e + 1) * ATT_HEAD_DIM].astype(BF16)


def attn_prep(att, q_gain, k_gain, cos_t, sin_t, bd):
    B, T, W = att.shape
    tt = _row_tile(T, ATT_KV_TILE)
    n_t = T // tt
    const = lambda shape: pl.BlockSpec(shape, lambda b, t: (0, 0))
    return pl.pallas_call(
        _attn_prep_kernel,
        grid=(B, n_t),
        in_specs=[pl.BlockSpec((1, tt, W), lambda b, t: (b, t, 0)), const((1, ATT_WIDTH)), const((1, ATT_KV_WIDTH)),
                  pl.BlockSpec((tt, LANES), lambda b, t: (t, 0)), pl.BlockSpec((tt, LANES), lambda b, t: (t, 0)),
                  const((ATT_WIDTH, ATT_WIDTH))],
        out_specs=[pl.BlockSpec((1, ATT_HEADS, tt, ATT_HEAD_DIM), lambda b, t: (b, 0, t, 0)),
                   pl.BlockSpec((1, ATT_KV_HEADS, 1, tt, ATT_HEAD_DIM), lambda b, t: (b, 0, t, 0, 0)),
                   pl.BlockSpec((1, ATT_KV_HEADS, 1, ATT_HEAD_DIM, tt), lambda b, t: (b, 0, t, 0, 0))],
        out_shape=[jax.ShapeDtypeStruct((B, ATT_HEADS, T, ATT_HEAD_DIM), BF16),
                   jax.ShapeDtypeStruct((B, ATT_KV_HEADS, n_t, tt, ATT_HEAD_DIM), BF16),
                   jax.ShapeDtypeStruct((B, ATT_KV_HEADS, n_t, ATT_HEAD_DIM, tt), BF16)],
        compiler_params=_params(("parallel", "parallel")),
        name="attn_prep",
    )(att, q_gain, k_gain, cos_t, sin_t, bd)


def _flash_kernel(q_ref, k_ref, vt_ref, o_ref, m_ref, l_ref, acc_ref, st0_ref, st1_ref, *, tq, n_kv):
    m_ref[...] = jnp.full_like(m_ref, -jnp.inf)
    l_ref[...] = jnp.zeros_like(l_ref)
    acc_ref[...] = jnp.zeros_like(acc_ref)
    cols = [slice(g * tq, (g + 1) * tq) for g in range(ATT_GROUP)]

    def score(st_ref, j, g):
        st_ref[:, cols[g]] = lax.dot_general(k_ref[0, 0, j], q_ref[0, g], (((1,), (1,)), ((), ())),
                                             preferred_element_type=F32)

    def consume(st_ref, j, g):
        c = cols[g]
        st = st_ref[:, c]
        m_prev = m_ref[:, c]
        m_new = jnp.maximum(m_prev, jnp.max(st, axis=0, keepdims=True))
        alpha = jnp.exp2(m_prev - m_new)
        p = jnp.exp2(st - m_new)
        l_ref[:, c] = alpha * l_ref[:, c] + jnp.sum(p, axis=0, keepdims=True)
        acc_ref[:, c] = alpha * acc_ref[:, c] + jnp.dot(vt_ref[0, 0, j], p.astype(BF16), preferred_element_type=F32)
        m_ref[:, c] = m_new

    for g in range(ATT_GROUP):
        score(st0_ref, 0, g)

    bufs = (st0_ref, st1_ref)
    unroll = 4 if n_kv % 4 == 0 else 2

    def kv_steps(i, carry):
        for u in range(unroll):
            j = unroll * i + u
            nxt = j + 1 if u < unroll - 1 else jnp.minimum(j + 1, n_kv - 1)
            for g in range(ATT_GROUP):
                score(bufs[(u + 1) % 2], nxt, g)
                consume(bufs[u % 2], j, g)
        return carry

    lax.fori_loop(0, n_kv // unroll, kv_steps, 0)
    if n_kv % 2:
        for g in range(ATT_GROUP):
            consume(st0_ref, n_kv - 1, g)
    o = (acc_ref[...] / l_ref[...]).T
    o_ref[0] = jnp.concatenate([o[c] for c in cols], axis=-1)


def flash_attention(q, k, vt):
    B, _, T, _ = q.shape
    n_kv, tk = k.shape[2], k.shape[3]
    tq = _row_tile(T, 256)
    rows = ATT_GROUP * tq
    return pl.pallas_call(
        functools.partial(_flash_kernel, tq=tq, n_kv=n_kv),
        grid=(B, ATT_KV_HEADS, T // tq),
        in_specs=[pl.BlockSpec((1, ATT_GROUP, tq, ATT_HEAD_DIM), lambda b, h, i: (b, h, i, 0)),
                  pl.BlockSpec((1, 1, n_kv, tk, ATT_HEAD_DIM), lambda b, h, i: (b, h, 0, 0, 0)),
                  pl.BlockSpec((1, 1, n_kv, ATT_HEAD_DIM, tk), lambda b, h, i: (b, h, 0, 0, 0))],
        out_specs=pl.BlockSpec((1, tq, ATT_GROUP * ATT_HEAD_DIM), lambda b, h, i: (b, i, h)),
        out_shape=jax.ShapeDtypeStruct((B, T, ATT_WIDTH), F32),
        scratch_shapes=[pltpu.VMEM((1, rows), F32), pltpu.VMEM((1, rows), F32), pltpu.VMEM((ATT_HEAD_DIM, rows), F32),
                        pltpu.VMEM((tk, rows), F32), pltpu.VMEM((tk, rows), F32)],
        compiler_params=_params(("parallel", "parallel", "arbitrary")),
        name="flash_attention",
    )(q, k, vt)


def _out_proj_kernel(x_ref, of_ref, ob_ref, z_ref, at_ref, gn_ref, w_ref, h_ref):
    o = of_ref[...] + ob_ref[...]
    z = z_ref[...]
    gn = gn_ref[...]
    parts = []
    for h in range(DN_HEADS):
        sl = slice(h * DN_HEAD_DIM, (h + 1) * DN_HEAD_DIM)
        zh = z[:, sl]
        parts.append(_rms(o[:, sl], gn) * (zh * jax.nn.sigmoid(zh)))
    parts.append(at_ref[...])
    mix = jnp.concatenate(parts, axis=-1).astype(BF16)
    h_ref[...] = x_ref[...] + jnp.dot(mix, w_ref[...], preferred_element_type=F32)


def out_proj(x, o_f, o_b, z, at, gn, w_out):
    n = x.shape[0]
    tm = _row_tile(n, 512)
    row = lambda w: pl.BlockSpec((tm, w), lambda i: (i, 0))
    return pl.pallas_call(
        _out_proj_kernel,
        grid=(n // tm,),
        in_specs=[row(D_MODEL), row(DN_WIDTH), row(DN_WIDTH), row(DN_WIDTH), row(ATT_WIDTH),
                  pl.BlockSpec((1, DN_HEAD_DIM), lambda i: (0, 0)), pl.BlockSpec((MIX_WIDTH, D_MODEL), lambda i: (0, 0))],
        out_specs=row(D_MODEL),
        out_shape=jax.ShapeDtypeStruct((n, D_MODEL), F32),
        compiler_params=_params(("parallel",)),
        name="out_proj",
    )(x, o_f, o_b, z, at, gn, w_out)


def _top_rows(s, ids, k):
    rows = lax.broadcasted_iota(jnp.int32, s.shape, 0)
    big = jnp.int32(s.shape[0])
    vals, outs = [], []
    for _ in range(k):
        m = jnp.max(s, axis=0, keepdims=True)
        pos = jnp.min(jnp.where(s == m, rows, big), axis=0, keepdims=True)
        hit = rows == pos
        vals.append(m)
        if ids is None:
            outs.append(pos)
        else:
            outs.append(jnp.max(jnp.where(hit, ids, -1), axis=0, keepdims=True))
        s = jnp.where(hit, -jnp.inf, s)
    return jnp.concatenate(vals, axis=0), jnp.concatenate(outs, axis=0)


def _stair_counts():
    return [PEER_TOPK // (i + 1) for i in range(PEER_TOPK)]


def _route_kernel(h_ref, g_ref, wq_ref, ka_ref, kb_ref, idx_ref, gate_ref):
    xn = _rms(h_ref[...], g_ref[...]).astype(BF16)
    q = jnp.dot(xn, wq_ref[...], preferred_element_type=F32).astype(BF16)
    nt = (((1,), (1,)), ((), ()))
    counts = _stair_counts()
    n_cand = sum(counts)
    pad = (-n_cand) % SUBLANES
    tt = q.shape[0]
    for h in range(PEER_HEADS):
        qa = q[:, (2 * h) * D_KEY_HALF:(2 * h + 1) * D_KEY_HALF]
        qb = q[:, (2 * h + 1) * D_KEY_HALF:(2 * h + 2) * D_KEY_HALF]
        sa = lax.dot_general(ka_ref[...], qa, nt, preferred_element_type=F32)
        sb = lax.dot_general(kb_ref[...], qb, nt, preferred_element_type=F32)
        va, ia = _top_rows(sa, None, PEER_TOPK)
        vb, ib = _top_rows(sb, None, PEER_TOPK)
        cs, ci = [], []
        for i, cnt in enumerate(counts):
            cs.append(va[i:i + 1] + vb[:cnt])
            ci.append(ia[i:i + 1] * N_KEYS + ib[:cnt])
        if pad:
            cs.append(jnp.full((pad, tt), -jnp.inf, F32))
            ci.append(jnp.zeros((pad, tt), jnp.int32))
        top_s, top_i = _top_rows(jnp.concatenate(cs, axis=0), jnp.concatenate(ci, axis=0), PEER_TOPK)
        e = jnp.exp(top_s - top_s[0:1])
        gate_ref[0, h * PEER_TOPK:(h + 1) * PEER_TOPK, :] = e / jnp.sum(e, axis=0, keepdims=True)
        idx_ref[0, h * PEER_TOPK:(h + 1) * PEER_TOPK, :] = top_i


def peer_route(h1, gain, wq, keys_a, keys_b):
    n = h1.shape[0]
    tt = _row_tile(n, 256)
    const = lambda shape: pl.BlockSpec(shape, lambda i: (0, 0))
    out = pl.BlockSpec((1, PEER_SEL, tt), lambda i: (i, 0, 0))
    return pl.pallas_call(
        _route_kernel,
        grid=(n // tt,),
        in_specs=[pl.BlockSpec((tt, D_MODEL), lambda i: (i, 0)), const((1, D_MODEL)),
                  const((D_MODEL, PEER_HEADS * 2 * D_KEY_HALF)), const((N_KEYS, D_KEY_HALF)), const((N_KEYS, D_KEY_HALF))],
        out_specs=[out, out],
        out_shape=[jax.ShapeDtypeStruct((n // tt, PEER_SEL, tt), jnp.int32),
                   jax.ShapeDtypeStruct((n // tt, PEER_SEL, tt), F32)],
        compiler_params=_params(("parallel",)),
        name="peer_route",
    )(h1, gain, wq, keys_a, keys_b)


def _peer_up_kernel(h_ref, g_ref, ut_ref, o_ref):
    xn = _rms(h_ref[...], g_ref[...]).astype(BF16)
    tm = xn.shape[0]
    for c2 in range(ut_ref.shape[1] // (2 * LANES)):
        res = jnp.dot(xn, ut_ref[:, c2 * 2 * LANES:(c2 + 1) * 2 * LANES], preferred_element_type=F32)
        o_ref[:, 2 * c2] = res[:, :LANES].reshape(tm // SUBLANES, SUBLANES, LANES)
        o_ref[:, 2 * c2 + 1] = res[:, LANES:].reshape(tm // SUBLANES, SUBLANES, LANES)


def peer_up(h1, gain, ut):
    n = h1.shape[0]
    tm = _row_tile(n, PEER_ROW_TILE)
    return pl.pallas_call(
        _peer_up_kernel,
        grid=(n // tm, N_EXPERTS // PEER_EXPERT_TILE),
        in_specs=[pl.BlockSpec((tm, D_MODEL), lambda i, j: (i, 0)), pl.BlockSpec((1, D_MODEL), lambda i, j: (0, 0)),
                  pl.BlockSpec((D_MODEL, PEER_EXPERT_TILE), lambda i, j: (0, j))],
        out_specs=pl.BlockSpec((tm // SUBLANES, PEER_EXPERT_TILE // LANES, SUBLANES, LANES), lambda i, j: (i, j, 0, 0)),
        out_shape=jax.ShapeDtypeStruct((n // SUBLANES, EXP_BLOCKS, SUBLANES, LANES), F32),
        compiler_params=_params(("parallel", "parallel")),
        name="peer_up",
    )(h1, gain, ut)


def _split_idx(idx):
    return (idx >> 7) & (SC_RANGE - 1), idx & (LANES - 1)


def _sc_worker_base(per_w):
    return (lax.axis_index("s") * SC_CORES + lax.axis_index("c")) * per_w


def sc_select(hd4, idx3):
    n_g = hd4.shape[0]
    per_w = n_g // SC_WORKERS
    assert per_w * SC_WORKERS == n_g
    n_rng = EXP_BLOCKS // SC_RANGE
    mesh = plsc.VectorSubcoreMesh(core_axis_name="c", subcore_axis_name="s")

    @functools.partial(
        pl.kernel, mesh=mesh, out_type=jax.ShapeDtypeStruct((n_g, SUBLANES, PEER_SEL), F32),
        scratch_types=[pltpu.VMEM((SUBLANES, PEER_SEL), jnp.int32), pltpu.VMEM((SUBLANES, PEER_SEL), F32),
                       pltpu.VMEM((SC_RANGE, SUBLANES, LANES), F32), pltpu.VMEM((SC_RANGE, SUBLANES, LANES), F32),
                       pltpu.SemaphoreType.DMA, pltpu.SemaphoreType.DMA],
        compiler_params=pltpu.CompilerParams(needs_layout_passes=False), name="peer_select",
    )
    def k(hd_hbm, idx_hbm, out_hbm, idx_v, sel_v, buf0, buf1, s0, s1):
        g0 = _sc_worker_base(per_w)
        bufs, sems = (buf0, buf1), (s0, s1)

        def fetch(g, r, b):
            return pltpu.make_async_copy(hd_hbm.at[g, pl.ds(r * SC_RANGE, SC_RANGE)], bufs[b], sems[b])

        @pl.loop(0, per_w)
        def _(i):
            g = g0 + i
            pltpu.sync_copy(idx_hbm.at[g], idx_v)
            fetch(g, 0, 0).start()
            for r in range(n_rng):
                b = r % 2
                if r + 1 < n_rng:
                    fetch(g, r + 1, 1 - b).start()
                fetch(g, r, b).wait()
                for s in range(SUBLANES):
                    srow = jnp.full((SC_LANES,), s, jnp.int32)
                    for v in range(PEER_SEL // SC_LANES):
                        vs = pl.ds(v * SC_LANES, SC_LANES)
                        idx = idx_v[s, vs]
                        blk, lane = _split_idx(idx)
                        val = plsc.load_gather(bufs[b], [blk, srow, lane])
                        hit = (idx >> 11) == r
                        sel_v[s, vs] = jnp.where(hit, val, 0.0 if r == 0 else sel_v[s, vs])
            pltpu.sync_copy(sel_v, out_hbm.at[g])

    return k(hd4, idx3)


def sc_expand(a3, idx3):
    n_g = a3.shape[0]
    per_w = n_g // SC_WORKERS
    assert per_w * SC_WORKERS == n_g
    n_rng = EXP_BLOCKS // SC_RANGE
    mesh = plsc.VectorSubcoreMesh(core_axis_name="c", subcore_axis_name="s")

    @functools.partial(
        pl.kernel, mesh=mesh, out_type=jax.ShapeDtypeStruct((n_g, EXP_BLOCKS, SUBLANES, LANES), F32),
        scratch_types=[pltpu.VMEM((SUBLANES, PEER_SEL), jnp.int32), pltpu.VMEM((SUBLANES, PEER_SEL), F32),
                       pltpu.VMEM((SC_RANGE, SUBLANES, LANES), F32), pltpu.VMEM((SC_RANGE, SUBLANES, LANES), F32),
                       pltpu.SemaphoreType.DMA, pltpu.SemaphoreType.DMA],
        compiler_params=pltpu.CompilerParams(needs_layout_passes=False), name="peer_expand",
    )
    def k(a_hbm, idx_hbm, out_hbm, idx_v, a_v, buf0, buf1, s0, s1):
        g0 = _sc_worker_base(per_w)
        bufs, sems = (buf0, buf1), (s0, s1)
        zero = jnp.zeros((SC_LANES,), F32)

        def put(g, r, b):
            return pltpu.make_async_copy(bufs[b], out_hbm.at[g, pl.ds(r * SC_RANGE, SC_RANGE)], sems[b])

        def touch(r, b, add):
            for s in range(SUBLANES):
                srow = jnp.full((SC_LANES,), s, jnp.int32)
                for v in range(PEER_SEL // SC_LANES):
                    vs = pl.ds(v * SC_LANES, SC_LANES)
                    idx = idx_v[s, vs]
                    blk, lane = _split_idx(idx)
                    hit = (idx >> 11) == r
                    if add:
                        plsc.addupdate_scatter(bufs[b], [blk, srow, lane], a_v[s, vs], mask=hit)
                    else:
                        plsc.store_scatter(bufs[b], [blk, srow, lane], zero, mask=hit)

        for b in range(2):
            @pl.loop(0, SC_RANGE)
            def _(blk):
                for s in range(SUBLANES):
                    for c in range(LANES // SC_LANES):
                        bufs[b][blk, s, pl.ds(c * SC_LANES, SC_LANES)] = zero

        @pl.loop(0, per_w)
        def _(i):
            g = g0 + i
            pltpu.sync_copy(idx_hbm.at[g], idx_v)
            pltpu.sync_copy(a_hbm.at[g], a_v)
            for r in range(n_rng):
                b = r % 2
                if r >= 2:
                    put(g, r - 2, b).wait()
                    touch(r - 2, b, add=False)
                touch(r, b, add=True)
                put(g, r, b).start()
            for r in range(n_rng - 2, n_rng):
                put(g, r, r % 2).wait()
                touch(r, r % 2, add=False)

    return k(a3, idx3)


def _peer_act_kernel(h_ref, gate_ref, o_ref):
    h = h_ref[...]
    o_ref[...] = 0.5 * h * (1.0 + lax.erf(h * math.sqrt(0.5))) * gate_ref[...]


def peer_act(hsel, gate):
    n = hsel.shape[0]
    tm = _row_tile(n, 2048)
    spec = pl.BlockSpec((tm, PEER_SEL), lambda i: (i, 0))
    return pl.pallas_call(
        _peer_act_kernel, grid=(n // tm,), in_specs=[spec, spec], out_specs=spec,
        out_shape=jax.ShapeDtypeStruct((n, PEER_SEL), F32),
        compiler_params=_params(("parallel",)), name="peer_act",
    )(hsel, gate)


def _peer_down_kernel(a_ref, v_ref, h_ref, o_ref, acc_ref):
    j = pl.program_id(1)

    @pl.when(j == 0)
    def _():
        acc_ref[...] = jnp.zeros_like(acc_ref)

    tm = acc_ref.shape[0]
    acc = acc_ref[...]
    for c2 in range(a_ref.shape[1] // 2):
        lhs = jnp.concatenate([a_ref[:, 2 * c2].reshape(tm, LANES), a_ref[:, 2 * c2 + 1].reshape(tm, LANES)], axis=-1)
        acc = acc + jnp.dot(lhs.astype(BF16), v_ref[c2 * 2 * LANES:(c2 + 1) * 2 * LANES, :], preferred_element_type=F32)
    acc_ref[...] = acc

    @pl.when(j == pl.num_programs(1) - 1)
    def _():
        o_ref[...] = h_ref[...] + acc_ref[...]


def peer_down(ad4, v, h1):
    n = h1.shape[0]
    tm = _row_tile(n, PEER_ROW_TILE)
    return pl.pallas_call(
        _peer_down_kernel,
        grid=(n // tm, N_EXPERTS // PEER_EXPERT_TILE),
        in_specs=[pl.BlockSpec((tm // SUBLANES, PEER_EXPERT_TILE // LANES, SUBLANES, LANES), lambda i, j: (i, j, 0, 0)),
                  pl.BlockSpec((PEER_EXPERT_TILE, D_MODEL), lambda i, j: (j, 0)),
                  pl.BlockSpec((tm, D_MODEL), lambda i, j: (i, 0))],
        out_specs=pl.BlockSpec((tm, D_MODEL), lambda i, j: (i, 0)),
        out_shape=jax.ShapeDtypeStruct((n, D_MODEL), F32),
        scratch_shapes=[pltpu.VMEM((tm, D_MODEL), F32)],
        compiler_params=_params(("parallel", "arbitrary")),
        name="peer_down",
    )(ad4, v, h1)


def _ple_kernel(h_ref, p_ref, wp_ref, pn_ref, wg_ref, fn_ref, y_ref):
    h = h_ref[...]
    ple = _rms(jnp.dot(p_ref[...].astype(BF16), wp_ref[...], preferred_element_type=F32), pn_ref[...])
    gate = jax.nn.sigmoid(jnp.dot(h.astype(BF16), wg_ref[...], preferred_element_type=F32))
    y_ref[...] = _rms(h + ple * gate, fn_ref[...])


def ple_final(h2, p, wp, pn, wg, fn):
    n = h2.shape[0]
    tm = _row_tile(n, 512)
    row = lambda w: pl.BlockSpec((tm, w), lambda i: (i, 0))
    const = lambda shape: pl.BlockSpec(shape, lambda i: (0, 0))
    return pl.pallas_call(
        _ple_kernel,
        grid=(n // tm,),
        in_specs=[row(D_MODEL), row(PLE_DIM), const((PLE_DIM, D_MODEL)), const((1, D_MODEL)),
                  const((D_MODEL, D_MODEL)), const((1, D_MODEL))],
        out_specs=row(D_MODEL),
        out_shape=jax.ShapeDtypeStruct((n, D_MODEL), F32),
        compiler_params=_params(("parallel",)),
        name="ple_final",
    )(h2, p, wp, pn, wg, fn)


def _rope_tables(T):
    quarter = ATT_HEAD_DIM // 4
    inv_freq = ROPE_THETA ** (-jnp.arange(0, 2 * quarter, 2, dtype=F32) / (2 * quarter))
    t = jnp.arange(T)
    row_pos = (t // GRID_W).astype(F32)
    col_pos = (t % GRID_W).astype(F32)
    ang_r = row_pos[:, None] * inv_freq[None, :]
    ang_c = col_pos[:, None] * inv_freq[None, :]
    ang = jnp.concatenate([ang_r, ang_r, ang_c, ang_c], axis=-1)
    sign = jnp.tile(jnp.concatenate([-jnp.ones((quarter,), F32), jnp.ones((quarter,), F32)]), 2)
    reps = LANES // ATT_HEAD_DIM
    return jnp.tile(jnp.cos(ang), (1, reps)), jnp.tile(jnp.sin(ang) * sign, (1, reps))


def _gate_rows(a_log_f, a_log_b, dtb_f, dtb_b):
    zeros = jnp.zeros((2 * DN_HEADS,), F32)
    tail = jnp.zeros((GATE_PAD - 4 * DN_HEADS,), F32)
    alog = jnp.concatenate([zeros, a_log_f.astype(F32), a_log_b.astype(F32), tail])[None, :]
    dtb = jnp.concatenate([zeros, dtb_f.astype(F32), dtb_b.astype(F32), tail])[None, :]
    return alog, dtb


def _layer_front(x, wts):
    B, T, D = x.shape
    n = B * T
    xf = x.reshape(n, D)
    qkv, z, gates, att = in_proj(xf, wts["attn_norm"], wts["w_in"])
    qkvc = dn_conv(qkv.reshape(B, T, -1), wts["conv_w"])
    gates3 = gates.reshape(B, T, -1)
    o_f = delta_scan(qkvc, gates3, wts["alog"], wts["dtb"], reverse=False)
    o_b = delta_scan(qkvc, gates3, wts["alog"], wts["dtb"], reverse=True)
    cos_t, sin_t = _rope_tables(T)
    q, k, v = attn_prep(att.reshape(B, T, -1), wts["q_norm"], wts["k_norm"], cos_t, sin_t, wts["head_avg"])
    at = flash_attention(q, k, v)
    h1 = out_proj(xf, o_f.reshape(n, -1), o_b.reshape(n, -1), z, at.reshape(n, -1), wts["dn_out_norm"], wts["w_out"])
    idx, gate = peer_route(h1, wts["ffn_norm"], wts["peer_query"], wts["keys_a"], wts["keys_b"])
    idx3 = idx.transpose(0, 2, 1).reshape(n // SUBLANES, SUBLANES, PEER_SEL)
    gate_t = gate.transpose(0, 2, 1).reshape(n, PEER_SEL)
    hsel = sc_select(peer_up(h1, wts["ffn_norm"], wts["peer_ut"]), idx3)
    return h1, hsel, idx3, gate_t


def _layer_mid(h1, hsel, idx3, gate_t):
    n = h1.shape[0]
    a3 = peer_act(hsel.reshape(n, PEER_SEL), gate_t).reshape(n // SUBLANES, SUBLANES, PEER_SEL)
    return h1, sc_expand(a3, idx3)


def _layer_back(h1, ad4, p, wts):
    n = h1.shape[0]
    h2 = peer_down(ad4, wts["peer_v"], h1)
    return ple_final(h2, p.reshape(n, -1), wts["ple_proj"], wts["ple_norm"], wts["ple_gate"], wts["final_norm"])


def _layer(x, p, wts):
    return _layer_back(*_layer_mid(*_layer_front(x, wts)), p, wts).reshape(x.shape)


def _prep_weights(attn_norm, w_in, conv_w, a_log_fwd, a_log_bwd, dt_bias_fwd, dt_bias_bwd, dn_out_norm, q_norm, k_norm, w_out, ffn_norm, peer_query, peer_keys_a, peer_keys_b, peer_u, peer_v, ple_proj, ple_norm, ple_gate, final_norm):
    c1 = 4 * DN_WIDTH
    n_gate = 4 * DN_HEADS
    w_pad = jnp.concatenate([w_in[:, :c1], w_in[:, c1:c1 + n_gate], jnp.zeros((D_MODEL, GATE_PAD - n_gate), w_in.dtype),
                             w_in[:, c1 + n_gate:]], axis=1).astype(BF16)
    conv_pad = jnp.concatenate([conv_w, jnp.zeros((SUBLANES - CONV_K, conv_w.shape[1]), conv_w.dtype)], axis=0).astype(F32)
    alog, dtb = _gate_rows(a_log_fwd, a_log_bwd, dt_bias_fwd, dt_bias_bwd)
    hid = np.arange(ATT_WIDTH) // ATT_HEAD_DIM
    head_avg = jnp.asarray((hid[:, None] == hid[None, :]).astype(np.float32) / ATT_HEAD_DIM)
    row = lambda g: g.astype(F32)[None, :]
    return dict(
        attn_norm=row(attn_norm), w_in=w_pad, conv_w=conv_pad, alog=alog, dtb=dtb, dn_out_norm=row(dn_out_norm),
        q_norm=jnp.tile(row(q_norm), (1, ATT_HEADS)), k_norm=jnp.tile(row(k_norm), (1, ATT_KV_HEADS)), head_avg=head_avg,
        w_out=w_out.astype(BF16), ffn_norm=row(ffn_norm), peer_query=peer_query.astype(BF16),
        keys_a=peer_keys_a.astype(BF16), keys_b=peer_keys_b.astype(BF16),
        peer_ut=peer_u.T.astype(BF16), peer_v=peer_v.astype(BF16),
        ple_proj=ple_proj.astype(BF16), ple_norm=row(ple_norm), ple_gate=ple_gate.astype(BF16), final_norm=row(final_norm))


def kernel(x_prompt, x_sample, p_prompt, p_sample, attn_norm, w_in, conv_w, a_log_fwd, a_log_bwd, dt_bias_fwd, dt_bias_bwd, dn_out_norm, q_norm, k_norm, w_out, ffn_norm, peer_query, peer_keys_a, peer_keys_b, peer_u, peer_v, ple_proj, ple_norm, ple_gate, final_norm):
    assert attn_norm.shape[0] == 1
    wts = _prep_weights(attn_norm[0], w_in[0], conv_w[0], a_log_fwd[0], a_log_bwd[0], dt_bias_fwd[0], dt_bias_bwd[0],
                        dn_out_norm[0], q_norm[0], k_norm[0], w_out[0], ffn_norm[0], peer_query[0], peer_keys_a[0],
                        peer_keys_b[0], peer_u[0], peer_v[0], ple_proj[0], ple_norm[0], ple_gate[0], final_norm)
    pieces = [(x_sample[b:b + 1], p_sample[0, b]) for b in range(x_sample.shape[0])]
    pieces += [(x_prompt[b:b + 1], p_prompt[0, b]) for b in range(x_prompt.shape[0])]
    fronts = [_layer_front(x, wts) for x, _ in pieces]
    mids = [_layer_mid(*f) for f in fronts]
    ys = [_layer_back(*m, p, wts) for m, (_, p) in zip(mids, pieces)]
    n_s = x_sample.shape[0]
    return jnp.stack(ys[n_s:], axis=0), jnp.stack(ys[:n_s], axis=0)
```
